```python
import jax, jax.numpy as jnp
from jax import lax
import numpy as np

D_MODEL = 2048
BATCH = 16
SEQ = 2048
DEPTH = 1
DEC_BATCH = 128
DEC_SEQ = 4
PAST_LEN = 16384
PAGE_SIZE = 128

N_HEADS = 16
N_KV_HEADS = 4
HEAD_DIM = 64
Q_GROUP = N_HEADS // N_KV_HEADS
ATTN_WIDTH = N_HEADS * HEAD_DIM
KV_WIDTH = N_KV_HEADS * HEAD_DIM
WINDOW = 128
ATTN_BLOCK = 128
ROPE_THETA = 10000.0
POOL_WINDOWS = (2, 4, 8, 16)
N_POOL_GROUPS = 4
POOL_WIDTH = D_MODEL // 2
POOL_GROUP = POOL_WIDTH // N_POOL_GROUPS
POOL_OUT_GROUP = D_MODEL // N_POOL_GROUPS
POOL_HIST = max(POOL_WINDOWS) - 1
IN_WIDTH = ATTN_WIDTH + 2 * KV_WIDTH + POOL_WIDTH + 2 * D_MODEL
PEER_HEADS = 8
N_KEYS = 128
N_EXPERTS = N_KEYS * N_KEYS
D_KEY = 256
HALF_KEY = D_KEY // 2
PEER_TOPK = 16
PEER_CHUNK = 128
ALPHA = (2.0 * DEPTH) ** 0.25
BETA = (8.0 * DEPTH) ** -0.25
LN_EPS = 1e-5
NEG_INF = -1e30

kernel_name = 'hybrid_pool_swa_peer_decode_step'


def _layer_norm(x, g, b):
    xf = x.astype(jnp.float32)
    mu = jnp.mean(xf, axis=-1, keepdims=True)
    xc = xf - mu
    var = jnp.mean(xc * xc, axis=-1, keepdims=True)
    return (xc * lax.rsqrt(var + LN_EPS) * g.astype(jnp.float32) + b.astype(jnp.float32)).astype(x.dtype)


def _rope(x, pos):
    half = HEAD_DIM // 2
    inv = ROPE_THETA ** (-jnp.arange(half, dtype=jnp.float32) / half)
    ang = pos.astype(jnp.float32)[:, None] * inv[None, :]
    cos = jnp.cos(ang)[:, None, :]
    sin = jnp.sin(ang)[:, None, :]
    xf = x.astype(jnp.float32)
    x1, x2 = xf[..., :half], xf[..., half:]
    return jnp.concatenate([x1 * cos - x2 * sin, x2 * cos + x1 * sin], axis=-1).astype(x.dtype)


def _in_proj(x, w_in, b_in):
    B, T, _ = x.shape
    z = jnp.einsum('btd,de->bte', x, w_in) + b_in
    o1 = ATTN_WIDTH
    o2 = o1 + KV_WIDTH
    o3 = o2 + KV_WIDTH
    o4 = o3 + POOL_WIDTH
    o5 = o4 + D_MODEL
    q = z[..., :o1].reshape(B, T, N_HEADS, HEAD_DIM)
    k = z[..., o1:o2].reshape(B, T, N_KV_HEADS, HEAD_DIM)
    v = z[..., o2:o3].reshape(B, T, N_KV_HEADS, HEAD_DIM)
    u = z[..., o3:o4]
    g_pool = z[..., o4:o5]
    g_attn = z[..., o5:]
    return q, k, v, u, g_pool, g_attn


def _sink_attention(q, k, v, q_pos, k_pos, sinks):
    s = jnp.einsum('...qkgd,...skd->...kgqs', q, k).astype(jnp.float32) * (HEAD_DIM ** -0.5)
    dist = q_pos[..., :, None] - k_pos[..., None, :]
    valid = (k_pos[..., None, :] >= 0) & (dist >= 0) & (dist < WINDOW)
    s = jnp.where(valid[..., None, None, :, :], s, NEG_INF)
    sink = sinks.astype(jnp.float32).reshape(N_KV_HEADS, Q_GROUP)[:, :, None, None]
    m = jnp.maximum(jnp.max(s, axis=-1, keepdims=True), sink)
    p = jnp.exp(s - m)
    denom = jnp.sum(p, axis=-1, keepdims=True) + jnp.exp(sink - m)
    p = (p / denom).astype(v.dtype)
    return jnp.einsum('...kgqs,...skd->...qkgd', p, v)


def _attn_prompt(q, k, v, sinks):
    B, T = q.shape[:2]
    nb = T // ATTN_BLOCK
    pos = jnp.arange(T, dtype=jnp.int32)
    q = _rope(q, pos)
    k = _rope(k, pos)
    qb = q.reshape(B, nb, ATTN_BLOCK, N_KV_HEADS, Q_GROUP, HEAD_DIM)

    def band(a):
        prev = jnp.pad(a, ((0, 0), (ATTN_BLOCK, 0), (0, 0), (0, 0)))[:, :T]
        prev = prev.reshape(B, nb, ATTN_BLOCK, N_KV_HEADS, HEAD_DIM)
        cur = a.reshape(B, nb, ATTN_BLOCK, N_KV_HEADS, HEAD_DIM)
        return jnp.concatenate([prev, cur], axis=2)

    kb, vb = band(k), band(v)
    start = jnp.arange(nb, dtype=jnp.int32)[:, None] * ATTN_BLOCK
    q_pos = (start + jnp.arange(ATTN_BLOCK, dtype=jnp.int32)[None, :])[None]
    k_pos = (start - ATTN_BLOCK + jnp.arange(2 * ATTN_BLOCK, dtype=jnp.int32)[None, :])[None]
    o = _sink_attention(qb, kb, vb, q_pos, k_pos, sinks)
    return o.reshape(B, T, ATTN_WIDTH), k, v


def _attn_sample(q, k, v, cache_k, cache_v, sinks):
    DB, T = q.shape[:2]
    L = cache_k.shape[1]
    pos = PAST_LEN + jnp.arange(T, dtype=jnp.int32)
    q = _rope(q, pos)
    k = _rope(k, pos)
    k_all = jnp.concatenate([cache_k, k], axis=1)
    v_all = jnp.concatenate([cache_v, v], axis=1)
    k_pos = jnp.concatenate([PAST_LEN - L + jnp.arange(L, dtype=jnp.int32), pos])[None]
    q_pos = pos[None]
    qg = q.reshape(DB, T, N_KV_HEADS, Q_GROUP, HEAD_DIM)
    o = _sink_attention(qg, k_all, v_all, q_pos, k_pos, sinks)
    return o.reshape(DB, T, ATTN_WIDTH), k_all[:, -L:], v_all[:, -L:]


def _pool_mix(u_all, n_hist, first_pos, w_pool, pool_scale):
    B, L, _ = u_all.shape
    T = L - n_hist
    uf = u_all.astype(jnp.float32)
    cs = jnp.pad(jnp.cumsum(uf, axis=1), ((0, 0), (1, 0), (0, 0)))
    pos = first_pos + jnp.arange(T, dtype=jnp.int32)
    outs = []
    for g, w in enumerate(POOL_WINDOWS):
        lo, hi = g * POOL_GROUP, (g + 1) * POOL_GROUP
        ext = jnp.pad(cs[..., lo:hi], ((0, 0), (w - 1, 0), (0, 0)))
        wsum = ext[:, n_hist + w:L + w] - ext[:, n_hist:L]
        cnt = jnp.minimum(w, pos + 1).astype(jnp.float32)[None, :, None]
        outs.append(wsum / cnt - uf[:, n_hist:, lo:hi])
    d = jnp.stack(outs, axis=2).astype(u_all.dtype)
    y = jnp.einsum('btgc,gce->btge', d, w_pool) * pool_scale
    return y.reshape(B, T, D_MODEL)


def _merge(pool_br, attn_heads, g_pool, g_attn, w_attn_br, w_out):
    attn_br = jnp.einsum('bta,ad->btd', attn_heads, w_attn_br)
    m = jax.nn.sigmoid(g_pool) * pool_br + jax.nn.sigmoid(g_attn) * attn_br
    return jnp.einsum('btd,de->bte', m, w_out)


def _peer(h, w_query, sub_keys, peer_u, peer_v):
    shp = h.shape
    hf = h.reshape(-1, D_MODEL)
    n = hf.shape[0]
    n_chunks = -(-n // PEER_CHUNK)
    pad = n_chunks * PEER_CHUNK - n
    hp = jnp.pad(hf, ((0, pad), (0, 0))).reshape(n_chunks, PEER_CHUNK, D_MODEL)

    def chunk(xc):
        q = (xc @ w_query).reshape(PEER_CHUNK, PEER_HEADS, 2, HALF_KEY)
        s = jnp.einsum('chpe,hpne->chpn', q, sub_keys).astype(jnp.float32)
        ts, ti = lax.top_k(s, PEER_TOPK)
        cand = ts[:, :, 0, :, None] + ts[:, :, 1, None, :]
        cs, ci = lax.top_k(cand.reshape(PEER_CHUNK, PEER_HEADS, PEER_TOPK * PEER_TOPK), PEER_TOPK)
        i1 = jnp.take_along_axis(ti[:, :, 0], ci // PEER_TOPK, axis=-1)
        i2 = jnp.take_along_axis(ti[:, :, 1], ci % PEER_TOPK, axis=-1)
        idx = i1 * N_KEYS + i2
        gate = jax.nn.softmax(cs, axis=-1)
        a = jnp.einsum('cd,chkd->chk', xc, peer_u[idx])
        act = (jax.nn.gelu(a.astype(jnp.float32), approximate=False) * gate).astype(xc.dtype)
        return jnp.einsum('chk,chkd->cd', act, peer_v[idx])

    out = lax.map(chunk, hp).reshape(-1, D_MODEL)[:n]
    return out.reshape(shp)


def _post_norm_block(x, tok, ln1_g, ln1_b, w_query, sub_keys, peer_u, peer_v, ln2_g, ln2_b):
    h = _layer_norm(ALPHA * x + tok, ln1_g, ln1_b)
    return _layer_norm(ALPHA * h + _peer(h, w_query, sub_keys, peer_u, peer_v), ln2_g, ln2_b)


def setup_inputs(seed: int = 0) -> dict:
    key = jax.random.key(seed)
    ks = jax.random.split(key, 24)
    f32 = jnp.float32
    buf = min(WINDOW, PAST_LEN)
    nrm = lambda k, shape, scale: jax.random.normal(k, shape, f32) * scale
    return {
        'x_prompt': nrm(ks[0], (BATCH, SEQ, D_MODEL), 1.0),
        'x_sample': nrm(ks[1], (DEC_BATCH, DEC_SEQ, D_MODEL), 1.0),
        'cache_k': nrm(ks[2], (DEC_BATCH, buf, N_KV_HEADS, HEAD_DIM), 1.0),
        'cache_v': nrm(ks[3], (DEC_BATCH, buf, N_KV_HEADS, HEAD_DIM), 1.0),
        'state_pool': nrm(ks[4], (DEC_BATCH, POOL_HIST, POOL_WIDTH), 1.0),
        'w_in': nrm(ks[5], (D_MODEL, IN_WIDTH), D_MODEL ** -0.5),
        'b_in': nrm(ks[6], (IN_WIDTH,), 0.02),
        'attn_sinks': nrm(ks[7], (N_HEADS,), 0.5),
        'w_attn_br': nrm(ks[8], (ATTN_WIDTH, D_MODEL), ATTN_WIDTH ** -0.5),
        'w_pool': nrm(ks[9], (N_POOL_GROUPS, POOL_GROUP, POOL_OUT_GROUP), POOL_GROUP ** -0.5),
        'pool_scale': 1.0 + nrm(ks[10], (N_POOL_GROUPS, POOL_OUT_GROUP), 0.1),
        'w_out': nrm(ks[11], (D_MODEL, D_MODEL), BETA * D_MODEL ** -0.5),
        'ln1_g': 1.0 + nrm(ks[12], (D_MODEL,), 0.02),
        'ln1_b': nrm(ks[13], (D_MODEL,), 0.02),
        'w_query': nrm(ks[14], (D_MODEL, PEER_HEADS * D_KEY), D_MODEL ** -0.5),
        'sub_keys': nrm(ks[15], (PEER_HEADS, 2, N_KEYS, HALF_KEY), HALF_KEY ** -0.5),
        'peer_u': nrm(ks[16], (N_EXPERTS, D_MODEL), D_MODEL ** -0.5),
        'peer_v': nrm(ks[17], (N_EXPERTS, D_MODEL), BETA),
        'ln2_g': 1.0 + nrm(ks[18], (D_MODEL,), 0.02),
        'ln2_b': nrm(ks[19], (D_MODEL,), 0.02),
    }


def reference(x_prompt, x_sample, cache_k, cache_v, state_pool, w_in, b_in, attn_sinks, w_attn_br,
              w_pool, pool_scale, w_out, ln1_g, ln1_b, w_query, sub_keys, peer_u, peer_v, ln2_g, ln2_b):
    x = x_prompt
    for _ in range(DEPTH):
        q, k, v, u, g_pool, g_attn = _in_proj(x, w_in, b_in)
        attn_p, k_rot, v_p = _attn_prompt(q, k, v, attn_sinks)
        pool_p = _pool_mix(u, 0, 0, w_pool, pool_scale)
        tok = _merge(pool_p, attn_p, g_pool, g_attn, w_attn_br, w_out)
        x = _post_norm_block(x, tok, ln1_g, ln1_b, w_query, sub_keys, peer_u, peer_v, ln2_g, ln2_b)
        L_p = min(WINDOW, x.shape[1])
        k_win_prompt = k_rot[:, -L_p:]
        v_win_prompt = v_p[:, -L_p:]
        pool_hist_prompt = u[:, -POOL_HIST:]
    y_prompt = x

    x = x_sample
    for _ in range(DEPTH):
        q, k, v, u, g_pool, g_attn = _in_proj(x, w_in, b_in)
        attn_s, k_win_sample, v_win_sample = _attn_sample(q, k, v, cache_k, cache_v, attn_sinks)
        u_all = jnp.concatenate([state_pool, u], axis=1)
        pool_s = _pool_mix(u_all, state_pool.shape[1], PAST_LEN, w_pool, pool_scale)
        tok = _merge(pool_s, attn_s, g_pool, g_attn, w_attn_br, w_out)
        x = _post_norm_block(x, tok, ln1_g, ln1_b, w_query, sub_keys, peer_u, peer_v, ln2_g, ln2_b)
        pool_hist_sample = u_all[:, -POOL_HIST:]
    y_sample = x

    return (y_prompt, y_sample, k_win_prompt, v_win_prompt, pool_hist_prompt, k_win_sample, v_win_sample, pool_hist_sample)
```

```python
import functools
import math

import jax
import jax.numpy as jnp
from jax import lax
from jax.experimental import pallas as pl
from jax.experimental.pallas import tpu as pltpu

D_MODEL = 2048
N_HEADS = 16
N_KV_HEADS = 4
HEAD_DIM = 64
Q_GROUP = N_HEADS // N_KV_HEADS
ATTN_WIDTH = N_HEADS * HEAD_DIM
KV_WIDTH = N_KV_HEADS * HEAD_DIM
WINDOW = 128
ATTN_BLOCK = 128
ROPE_THETA = 10000.0
POOL_WINDOWS = (2, 4, 8, 16)
N_POOL_GROUPS = 4
POOL_WIDTH = D_MODEL // 2
POOL_GROUP = POOL_WIDTH // N_POOL_GROUPS
POOL_OUT_GROUP = D_MODEL // N_POOL_GROUPS
POOL_HIST = max(POOL_WINDOWS) - 1
HIST_ROWS = 16
IN_WIDTH = ATTN_WIDTH + 2 * KV_WIDTH + POOL_WIDTH + 2 * D_MODEL
COL = 512
PEER_HEADS = 8
N_KEYS = 128
N_EXPERTS = N_KEYS * N_KEYS
HALF_KEY = 128
PEER_TOPK = 16
PEER_PICKS = PEER_HEADS * PEER_TOPK
DEPTH = 1
ALPHA = (2.0 * DEPTH) ** 0.25
LN_EPS = 1e-5
NEG_INF = -1e30
PAST_LEN = 16384

VMEM_LIMIT = 56 * 1024 * 1024
BF16 = jnp.bfloat16
F32 = jnp.float32


def _cparams(sem):
    return pltpu.CompilerParams(dimension_semantics=sem, vmem_limit_bytes=VMEM_LIMIT)


def _resident(shape):
    zeros = (0,) * len(shape)
    return pl.BlockSpec(shape, lambda i: zeros, pipeline_mode=pl.Buffered(1))


def _in_proj_kernel(x_ref, w_ref, b_ref, o_ref):
    x = x_ref[...].astype(BF16)
    o_ref[...] = jnp.dot(x, w_ref[...], preferred_element_type=F32) + b_ref[...]


def _in_proj(x, w_bf, b, tm):
    m, k = x.shape
    n = w_bf.shape[1]
    return pl.pallas_call(
        _in_proj_kernel,
        out_shape=jax.ShapeDtypeStruct((m, n), F32),
        grid=(m // tm, n // COL),
        in_specs=[pl.BlockSpec((tm, k), lambda i, j: (i, 0)),
                  pl.BlockSpec((k, COL), lambda i, j: (0, j)),
                  pl.BlockSpec((1, COL), lambda i, j: (0, j))],
        out_specs=pl.BlockSpec((tm, COL), lambda i, j: (i, j)),
        compiler_params=_cparams(("arbitrary", "arbitrary")),
        name="in_proj",
    )(x, w_bf, b.reshape(1, n))


def _rope_tables(pos):
    half = HEAD_DIM // 2
    inv = ROPE_THETA ** (-jnp.arange(half, dtype=F32) / half)
    ang = pos.astype(F32)[:, None] * inv[None, :]
    cos = jnp.cos(ang)
    sin = jnp.sin(ang)
    cos_t = jnp.concatenate([cos, cos, cos, cos], axis=1)
    sin_t = jnp.concatenate([-sin, sin, -sin, sin], axis=1)
    return cos_t, sin_t


def _rope_apply(x, cos_t, sin_t):
    rows, w = x.shape
    reps = w // 128
    cos_f = jnp.concatenate([cos_t] * reps, axis=1) if reps > 1 else cos_t
    sin_f = jnp.concatenate([sin_t] * reps, axis=1) if reps > 1 else sin_t
    lane = lax.broadcasted_iota(jnp.int32, (rows, w), 1)
    first_half = (lane % HEAD_DIM) < (HEAD_DIM // 2)
    partner = jnp.where(first_half, pltpu.roll(x, w - HEAD_DIM // 2, axis=1),
                        pltpu.roll(x, HEAD_DIM // 2, axis=1))
    return x * cos_f + partner * sin_f


def _attn_kernel(sinks_ref, q_ref, kc_ref, kp_ref, vc_ref, vp_ref, cos_ref, sin_ref, cosp_ref, sinp_ref,
                 o_ref, krot_ref, *, blocks_per_seq, rotate_prev):
    j = pl.program_id(0) % blocks_per_seq
    has_prev = jnp.logical_or(j > 0, not rotate_prev)
    cos_t = cos_ref[...]
    sin_t = sin_ref[...]
    q = _rope_apply(q_ref[...], cos_t, sin_t)
    kc = _rope_apply(kc_ref[...], cos_t, sin_t)
    krot_ref[...] = kc
    if rotate_prev:
        kp = _rope_apply(kp_ref[...], cosp_ref[...], sinp_ref[...])
    else:
        kp = kp_ref[...]
    vc = vc_ref[...]
    vp = vp_ref[...]

    rows = Q_GROUP * ATTN_BLOCK
    qi = lax.broadcasted_iota(jnp.int32, (rows, 2 * ATTN_BLOCK), 0) % ATTN_BLOCK
    kj = lax.broadcasted_iota(jnp.int32, (rows, 2 * ATTN_BLOCK), 1)
    prev_lim = jnp.where(has_prev, ATTN_BLOCK, 0)
    valid = jnp.logical_or(jnp.logical_and(kj < prev_lim, kj > qi),
                           jnp.logical_and(kj >= ATTN_BLOCK, (kj - ATTN_BLOCK) <= qi))
    head_of_row = lax.broadcasted_iota(jnp.int32, (rows, 1), 0) // ATTN_BLOCK

    for g in range(N_KV_HEADS):
        lo = g * HEAD_DIM
        k_band = jnp.concatenate([kp[:, lo:lo + HEAD_DIM], kc[:, lo:lo + HEAD_DIM]], axis=0).astype(BF16)
        v_band = jnp.concatenate([vp[:, lo:lo + HEAD_DIM], vc[:, lo:lo + HEAD_DIM]], axis=0).astype(BF16)
        q_g = jnp.concatenate(
            [q[:, (g * Q_GROUP + h) * HEAD_DIM:(g * Q_GROUP + h + 1) * HEAD_DIM] for h in range(Q_GROUP)],
            axis=0).astype(BF16)
        s = lax.dot_general(q_g, k_band, (((1,), (1,)), ((), ())), preferred_element_type=F32)
        s = s * (HEAD_DIM ** -0.5)
        s = jnp.where(valid, s, NEG_INF)
        sink = jnp.zeros((rows, 1), F32)
        for h in range(Q_GROUP):
            sink = jnp.where(head_of_row == h, sinks_ref[g * Q_GROUP + h], sink)
        m = jnp.maximum(jnp.max(s, axis=-1, keepdims=True), sink)
        p = jnp.exp(s - m)
        denom = jnp.sum(p, axis=-1, keepdims=True) + jnp.exp(sink - m)
        p = (p / denom).astype(BF16)
        o = jnp.dot(p, v_band, preferred_element_type=F32)
        for h in range(Q_GROUP):
            c0 = (g * Q_GROUP + h) * HEAD_DIM
            o_ref[:, c0:c0 + HEAD_DIM] = o[h * ATTN_BLOCK:(h + 1) * ATTN_BLOCK, :]


def _attention(sinks, q_src, k_src, kp_src, v_src, vp_src, cos_t, sin_t, *, n_seq, blocks_per_seq,
               q_col, k_col, kp_col, v_col, vp_col, rotate_prev):
    nb = blocks_per_seq
    n_blocks = n_seq * nb
    blk = ATTN_BLOCK

    def prev_blk(i):
        if rotate_prev:
            return jnp.where(i % nb == 0, i, i - 1)
        return i

    def pos_blk(i):
        return i % nb

    def pos_prev_blk(i):
        return jnp.maximum(i % nb - 1, 0)

    kern = functools.partial(_attn_kernel, blocks_per_seq=nb, rotate_prev=rotate_prev)
    return pl.pallas_call(
        kern,
        out_shape=(jax.ShapeDtypeStruct((n_blocks * blk, ATTN_WIDTH), F32),
                   jax.ShapeDtypeStruct((n_seq * blk, KV_WIDTH), F32)),
        grid=(n_blocks,),
        in_specs=[pl.BlockSpec(memory_space=pltpu.SMEM),
                  pl.BlockSpec((blk, ATTN_WIDTH), lambda i: (i, q_col)),
                  pl.BlockSpec((blk, KV_WIDTH), lambda i: (i, k_col)),
                  pl.BlockSpec((blk, KV_WIDTH), lambda i: (prev_blk(i), kp_col)),
                  pl.BlockSpec((blk, KV_WIDTH), lambda i: (i, v_col)),
                  pl.BlockSpec((blk, KV_WIDTH), lambda i: (prev_blk(i), vp_col)),
                  pl.BlockSpec((blk, 128), lambda i: (pos_blk(i), 0)),
                  pl.BlockSpec((blk, 128), lambda i: (pos_blk(i), 0)),
                  pl.BlockSpec((blk, 128), lambda i: (pos_prev_blk(i), 0)),
                  pl.BlockSpec((blk, 128), lambda i: (pos_prev_blk(i), 0))],
        out_specs=(pl.BlockSpec((blk, ATTN_WIDTH), lambda i: (i, 0)),
                   pl.BlockSpec((blk, KV_WIDTH), lambda i: (i // nb, 0))),
        compiler_params=_cparams(("arbitrary",)),
        name="sink_attention",
    )(sinks, q_src, k_src, kp_src, v_src, vp_src, cos_t, sin_t, cos_t, sin_t)


def _pool_diff_kernel(hlo_ref, hhi_ref, clo_ref, chi_ref, o_ref, ext_ref, *, tiles_per_seq, first_pos, mask_first):
    j = pl.program_id(0) % tiles_per_seq
    bb, tm, _ = clo_ref.shape
    hist = jnp.concatenate([hlo_ref[...], hhi_ref[...]], axis=2)
    if mask_first:
        hist = jnp.where(j > 0, hist, 0.0)
    cur = jnp.concatenate([clo_ref[...], chi_ref[...]], axis=2)
    ext_ref[:, 0:HIST_ROWS, :] = hist
    ext_ref[:, HIST_ROWS:HIST_ROWS + tm, :] = cur
    pos = first_pos + j * tm + lax.broadcasted_iota(jnp.int32, (1, tm, 1), 1)
    for g, w in enumerate(POOL_WINDOWS):
        lo = g * POOL_GROUP
        wsum = ext_ref[:, HIST_ROWS:HIST_ROWS + tm, lo:lo + POOL_GROUP]
        for i in range(1, w):
            wsum = wsum + ext_ref[:, HIST_ROWS - i:HIST_ROWS - i + tm, lo:lo + POOL_GROUP]
        cnt = jnp.minimum(w, pos + 1).astype(F32)
        o_ref[:, :, lo:lo + POOL_GROUP] = wsum / cnt - ext_ref[:, HIST_ROWS:HIST_ROWS + tm, lo:lo + POOL_GROUP]


def _pool_diff(hist_src, cur_src, *, n_seq, seq_len, bb, tm, col0, first_pos, hist_from_cur):
    tiles = seq_len // tm
    hb = tm // HIST_ROWS if hist_from_cur else 1

    def hist_map(c):
        if hist_from_cur:
            return lambda i: (i // tiles, jnp.maximum((i % tiles) * hb - 1, 0), c)
        return lambda i: (i, 0, c)

    def cur_map(c):
        return lambda i: (i // tiles, i % tiles, c)

    hcol = col0 if hist_from_cur else 0
    kern = functools.partial(_pool_diff_kernel, tiles_per_seq=tiles, first_pos=first_pos,
                             mask_first=hist_from_cur)
    return pl.pallas_call(
        kern,
        out_shape=jax.ShapeDtypeStruct((n_seq, seq_len, POOL_WIDTH), F32),
        grid=((n_seq // bb) * tiles,),
        in_specs=[pl.BlockSpec((bb, HIST_ROWS, COL), hist_map(hcol)),
                  pl.BlockSpec((bb, HIST_ROWS, COL), hist_map(hcol + 1)),
                  pl.BlockSpec((bb, tm, COL), cur_map(col0)),
                  pl.BlockSpec((bb, tm, COL), cur_map(col0 + 1))],
        out_specs=pl.BlockSpec((bb, tm, POOL_WIDTH), lambda i: (i // tiles, i % tiles, 0)),
        scratch_shapes=[pltpu.VMEM((bb, HIST_ROWS + tm, POOL_WIDTH), F32)],
        compiler_params=_cparams(("arbitrary",)),
        name="pool_diff",
    )(hist_src, hist_src, cur_src, cur_src)


def _layer_norm_rows(x, g, b):
    mu = jnp.mean(x, axis=-1, keepdims=True)
    xc = x - mu
    var = jnp.mean(xc * xc, axis=-1, keepdims=True)
    return xc * lax.rsqrt(var + LN_EPS) * g + b


def _merge_kernel(*refs):
    (d_ref, a_ref, gp0, gp1, gp2, gp3, ga0, ga1, ga2, ga3, x_ref,
     wp_ref, ps_ref, wab_ref, wo_ref, g_ref, b_ref, h_ref) = refs
    gp = (gp0, gp1, gp2, gp3)
    ga = (ga0, ga1, ga2, ga3)
    attn = a_ref[...].astype(BF16)
    tok = None
    for c in range(N_POOL_GROUPS):
        d_c = d_ref[:, c * POOL_GROUP:(c + 1) * POOL_GROUP].astype(BF16)
        y_pool = jnp.dot(d_c, wp_ref[c], preferred_element_type=F32) * ps_ref[:, c * COL:(c + 1) * COL]
        attn_br = jnp.dot(attn, wab_ref[:, c * COL:(c + 1) * COL], preferred_element_type=F32)
        m_c = jax.nn.sigmoid(gp[c][...]) * y_pool + jax.nn.sigmoid(ga[c][...]) * attn_br
        part = jnp.dot(m_c.astype(BF16), wo_ref[c * COL:(c + 1) * COL, :], preferred_element_type=F32)
        tok = part if tok is None else tok + part
    h_ref[...] = _layer_norm_rows(ALPHA * x_ref[...] + tok, g_ref[...], b_ref[...])


def _merge(d, attn, z, x, wp_bf, pool_scale, wab_bf, wo_bf, ln_g, ln_b, tm):
    m = x.shape[0]
    gp_col = (ATTN_WIDTH + 2 * KV_WIDTH + POOL_WIDTH) // COL
    ga_col = gp_col + D_MODEL // COL
    z_specs = [pl.BlockSpec((tm, COL), (lambda i, c=c: (i, c))) for c in range(gp_col, gp_col + 4)]
    z_specs += [pl.BlockSpec((tm, COL), (lambda i, c=c: (i, c))) for c in range(ga_col, ga_col + 4)]
    return pl.pallas_call(
        _merge_kernel,
        out_shape=jax.ShapeDtypeStruct((m, D_MODEL), F32),
        grid=(m // tm,),
        in_specs=[pl.BlockSpec((tm, POOL_WIDTH), lambda i: (i, 0)),
                  pl.BlockSpec((tm, ATTN_WIDTH), lambda i: (i, 0))] + z_specs + [
                  pl.BlockSpec((tm, D_MODEL), lambda i: (i, 0)),
                  _resident((N_POOL_GROUPS, POOL_GROUP, POOL_OUT_GROUP)),
                  _resident((1, D_MODEL)),
                  _resident((ATTN_WIDTH, D_MODEL)),
                  _resident((D_MODEL, D_MODEL)),
                  _resident((1, D_MODEL)),
                  _resident((1, D_MODEL))],
        out_specs=pl.BlockSpec((tm, D_MODEL), lambda i: (i, 0)),
        compiler_params=_cparams(("arbitrary",)),
        name="merge_ln1",
    )(d, attn, *([z] * 8), x, wp_bf, pool_scale.reshape(1, D_MODEL), wab_bf, wo_bf,
      ln_g.reshape(1, D_MODEL), ln_b.reshape(1, D_MODEL))


def _topk_rows(s, payload, k):
    n, t = s.shape
    row = lax.broadcasted_iota(jnp.int32, (n, t), 0).astype(F32)
    slot = lax.broadcasted_iota(jnp.int32, (k, t), 0)

    def body(j, carry):
        s, vals, picked = carry
        m = jnp.max(s, axis=0, keepdims=True)
        am = jnp.min(jnp.where(s == m, row, float(n)), axis=0, keepdims=True)
        hit = row == am
        pay = jnp.max(jnp.where(hit, payload, -1.0), axis=0, keepdims=True)
        vals = jnp.where(slot == j, m, vals)
        picked = jnp.where(slot == j, pay, picked)
        s = jnp.where(hit, -jnp.inf, s)
        return s, vals, picked

    _, vals, picked = lax.fori_loop(0, k, body, (s, jnp.zeros((k, t), F32), jnp.zeros((k, t), F32)))
    return vals, picked


def _route_kernel(h_ref, wq_ref, keys_ref, idx_ref, gate_ref, q_scr):
    tm = h_ref.shape[0]
    q_scr[...] = jnp.dot(h_ref[...].astype(BF16), wq_ref[...], preferred_element_type=F32).astype(BF16)
    key_id = lax.broadcasted_iota(jnp.int32, (N_KEYS, tm), 0).astype(F32)
    for hd in range(PEER_HEADS):
        tops = []
        for p in range(2):
            c0 = (hd * 2 + p) * HALF_KEY
            s = lax.dot_general(keys_ref[hd * 2 + p], q_scr[:, c0:c0 + HALF_KEY],
                                (((1,), (1,)), ((), ())), preferred_element_type=F32)
            tops.append(_topk_rows(s, key_id, PEER_TOPK))
        (ts1, ti1), (ts2, ti2) = tops
        cand = jnp.concatenate([ts1[a:a + 1, :] + ts2 for a in range(PEER_TOPK)], axis=0)
        eid = jnp.concatenate([ti1[a:a + 1, :] * float(N_KEYS) + ti2 for a in range(PEER_TOPK)], axis=0)
        cs, ids = _topk_rows(cand, eid, PEER_TOPK)
        e = jnp.exp(cs - jnp.max(cs, axis=0, keepdims=True))
        gate = e / jnp.sum(e, axis=0, keepdims=True)
        idx_ref[hd * PEER_TOPK:(hd + 1) * PEER_TOPK, :] = ids.astype(jnp.int32)
        gate_ref[hd * PEER_TOPK:(hd + 1) * PEER_TOPK, :] = gate


def _route(h, wq_bf, keys_bf, tm):
    m = h.shape[0]
    return pl.pallas_call(
        _route_kernel,
        out_shape=(jax.ShapeDtypeStruct((PEER_PICKS, m), jnp.int32),
                   jax.ShapeDtypeStruct((PEER_PICKS, m), F32)),
        grid=(m // tm,),
        in_specs=[pl.BlockSpec((tm, D_MODEL), lambda i: (i, 0)),
                  _resident((D_MODEL, PEER_HEADS * 2 * HALF_KEY)),
                  _resident((PEER_HEADS * 2, N_KEYS, HALF_KEY))],
        out_specs=(pl.BlockSpec((PEER_PICKS, tm), lambda i: (0, i)),
                   pl.BlockSpec((PEER_PICKS, tm), lambda i: (0, i))),
        scratch_shapes=[pltpu.VMEM((tm, PEER_HEADS * 2 * HALF_KEY), BF16)],
        compiler_params=_cparams(("arbitrary",)),
        name="peer_route",
    )(h, wq_bf, keys_bf)


PEER_TT = 8
ISSUE_UNROLL = 8


def _peer_kernel(idx_ref, idxn_ref, h_ref, gate_ref, g_ref, b_ref, uv_hbm, y_ref, buf, sem, *, n_steps):
    i = pl.program_id(0)
    slot = i % 2
    rows = PEER_TT * PEER_PICKS

    def issue(ids_ref, dst_slot):
        def body(kk, carry):
            for t in range(PEER_TT):
                for u in range(ISSUE_UNROLL):
                    k = kk * ISSUE_UNROLL + u
                    e = ids_ref[t, k]
                    pltpu.make_async_copy(uv_hbm.at[pl.ds(e, 1)],
                                          buf.at[dst_slot, pl.ds(t * PEER_PICKS + k, 1)],
                                          sem.at[dst_slot]).start()
            return carry
        lax.fori_loop(0, PEER_PICKS // ISSUE_UNROLL, body, 0)

    @pl.when(i == 0)
    def _():
        issue(idx_ref, 0)

    @pl.when(i + 1 < n_steps)
    def _():
        issue(idxn_ref, 1 - slot)

    pltpu.make_async_copy(uv_hbm.at[pl.ds(0, rows)], buf.at[slot], sem.at[slot]).wait()

    col0 = (i * PEER_TT) % 128
    lane = lax.broadcasted_iota(jnp.int32, (PEER_PICKS, 128), 1)
    gates = gate_ref[...]

    def token(t, carry):
        r0 = pl.multiple_of(t * PEER_PICKS, PEER_PICKS)
        x = h_ref[pl.ds(t, 1), :]
        u = buf[slot, pl.ds(r0, PEER_PICKS), 0:D_MODEL]
        a = jnp.sum(u * x, axis=1, keepdims=True)
        gate = jnp.sum(jnp.where(lane == col0 + t, gates, 0.0), axis=1, keepdims=True)
        act = 0.5 * a * (1.0 + lax.erf(a * (2.0 ** -0.5))) * gate
        v = buf[slot, pl.ds(r0, PEER_PICKS), D_MODEL:2 * D_MODEL]
        out = jnp.sum(act * v, axis=0, keepdims=True)
        y_ref[pl.ds(t, 1), :] = _layer_norm_rows(ALPHA * x + out, g_ref[...], b_ref[...])
        return carry

    lax.fori_loop(0, PEER_TT, token, 0)


def _peer_experts(idx_t, gate_t, h, uv, ln_g, ln_b):
    m = h.shape[0]
    n_steps = m // PEER_TT
    kern = functools.partial(_peer_kernel, n_steps=n_steps)
    return pl.pallas_call(
        kern,
        out_shape=jax.ShapeDtypeStruct((m, D_MODEL), F32),
        grid=(n_steps,),
        in_specs=[pl.BlockSpec((PEER_TT, PEER_PICKS), lambda i: (i, 0), memory_space=pltpu.SMEM),
                  pl.BlockSpec((PEER_TT, PEER_PICKS), lambda i: (jnp.minimum(i + 1, n_steps - 1), 0),
                               memory_space=pltpu.SMEM),
                  pl.BlockSpec((PEER_TT, D_MODEL), lambda i: (i, 0)),
                  pl.BlockSpec((PEER_PICKS, 128), lambda i: (0, (i * PEER_TT) // 128)),
                  pl.BlockSpec((1, D_MODEL), lambda i: (0, 0)),
                  pl.BlockSpec((1, D_MODEL), lambda i: (0, 0)),
                  pl.BlockSpec(memory_space=pl.ANY)],
        out_specs=pl.BlockSpec((PEER_TT, D_MODEL), lambda i: (i, 0)),
        scratch_shapes=[pltpu.VMEM((2, PEER_TT * PEER_PICKS, 2 * D_MODEL), F32),
                        pltpu.SemaphoreType.DMA((2,))],
        compiler_params=_cparams(("arbitrary",)),
        name="peer_experts",
    )(idx_t, idx_t, h, gate_t, ln_g.reshape(1, D_MODEL), ln_b.reshape(1, D_MODEL), uv)


def _token_mixer_and_peer(d, attn, z, x, weights, tm):
    (wp_bf, pool_scale, wab_bf, wo_bf, ln1_g, ln1_b, wq_bf, keys_bf, uv, ln2_g, ln2_b) = weights
    tm = min(tm, x.shape[0])
    h = _merge(d, attn, z, x, wp_bf, pool_scale, wab_bf, wo_bf, ln1_g, ln1_b, tm)
    ids, gate_t = _route(h, wq_bf, keys_bf, tm)
    return _peer_experts(ids.T, gate_t, h, uv, ln2_g, ln2_b)


def kernel(x_prompt, x_sample, cache_k, cache_v, state_pool, w_in, b_in, attn_sinks, w_attn_br, w_pool, pool_scale, w_out, ln1_g, ln1_b, w_query, sub_keys, peer_u, peer_v, ln2_g, ln2_b):
    bsz, seq, _ = x_prompt.shape
    dbs, dseq, _ = x_sample.shape
    win = cache_k.shape[1]
    u_col = (ATTN_WIDTH + 2 * KV_WIDTH) // COL

    w_in_bf = w_in.astype(BF16)
    weights = (w_pool.astype(BF16), pool_scale, w_attn_br.astype(BF16), w_out.astype(BF16), ln1_g, ln1_b,
               w_query.astype(BF16), sub_keys.reshape(PEER_HEADS * 2, N_KEYS, HALF_KEY).astype(BF16),
               jnp.concatenate([peer_u, peer_v], axis=1), ln2_g, ln2_b)

    xp = x_prompt.reshape(bsz * seq, D_MODEL)
    z = _in_proj(xp, w_in_bf, b_in, min(512, bsz * seq))
    cos_t, sin_t = _rope_tables(jnp.arange(seq, dtype=jnp.int32))
    attn_p, k_last = _attention(attn_sinks, z, z, z, z, z, cos_t, sin_t, n_seq=bsz,
                                blocks_per_seq=seq // ATTN_BLOCK, q_col=0,
                                k_col=ATTN_WIDTH // KV_WIDTH, kp_col=ATTN_WIDTH // KV_WIDTH,
                                v_col=ATTN_WIDTH // KV_WIDTH + 1, vp_col=ATTN_WIDTH // KV_WIDTH + 1,
                                rotate_prev=True)
    z3 = z.reshape(bsz, seq, IN_WIDTH)
    d_p = _pool_diff(z3, z3, n_seq=bsz, seq_len=seq, bb=1, tm=256, col0=u_col, first_pos=0,
                     hist_from_cur=True)
    y_prompt = _token_mixer_and_peer(d_p.reshape(bsz * seq, POOL_WIDTH), attn_p, z, xp, weights, 256)
    y_prompt = y_prompt.reshape(bsz, seq, D_MODEL)
    k_win_prompt = k_last.reshape(bsz, ATTN_BLOCK, N_KV_HEADS, HEAD_DIM)
    v_win_prompt = z3[:, seq - WINDOW:, ATTN_WIDTH + KV_WIDTH:ATTN_WIDTH + 2 * KV_WIDTH].reshape(
        bsz, WINDOW, N_KV_HEADS, HEAD_DIM)
    pool_hist_prompt = z3[:, seq - POOL_HIST:, u_col * COL:u_col * COL + POOL_WIDTH]

    xs = x_sample.reshape(dbs * dseq, D_MODEL)
    zs = _in_proj(xs, w_in_bf, b_in, dbs * dseq)
    zs3 = zs.reshape(dbs, dseq, IN_WIDTH)
    qkv_pad = jnp.pad(zs3[:, :, :ATTN_WIDTH + 2 * KV_WIDTH], ((0, 0), (0, ATTN_BLOCK - dseq), (0, 0)))
    qkv_pad = qkv_pad.reshape(dbs * ATTN_BLOCK, ATTN_WIDTH + 2 * KV_WIDTH)
    q_pad = qkv_pad[:, :ATTN_WIDTH]
    kv_pad = qkv_pad[:, ATTN_WIDTH:]
    ck = cache_k.reshape(dbs * win, KV_WIDTH)
    cv = cache_v.reshape(dbs * win, KV_WIDTH)
    cos_s, sin_s = _rope_tables(PAST_LEN + jnp.arange(ATTN_BLOCK, dtype=jnp.int32))
    attn_s, k_new = _attention(attn_sinks, q_pad, kv_pad, ck, kv_pad, cv, cos_s, sin_s, n_seq=dbs,
                               blocks_per_seq=1, q_col=0, k_col=0, kp_col=0, v_col=1, vp_col=0,
                               rotate_prev=False)
    attn_s = attn_s.reshape(dbs, ATTN_BLOCK, ATTN_WIDTH)[:, :dseq].reshape(dbs * dseq, ATTN_WIDTH)
    k_new = k_new.reshape(dbs, ATTN_BLOCK, N_KV_HEADS, HEAD_DIM)[:, :dseq]
    v_new = zs3[:, :, ATTN_WIDTH + KV_WIDTH:ATTN_WIDTH + 2 * KV_WIDTH].reshape(dbs, dseq, N_KV_HEADS, HEAD_DIM)
    k_win_sample = jnp.concatenate([cache_k, k_new], axis=1)[:, -win:]
    v_win_sample = jnp.concatenate([cache_v, v_new], axis=1)[:, -win:]

    u_s = zs3[:, :, u_col * COL:u_col * COL + POOL_WIDTH]
    hist_s = jnp.pad(state_pool, ((0, 0), (HIST_ROWS - POOL_HIST, 0), (0, 0)))
    d_s = _pool_diff(hist_s, u_s, n_seq=dbs, seq_len=dseq, bb=8, tm=dseq, col0=0, first_pos=PAST_LEN,
                     hist_from_cur=False)
    y_sample = _token_mixer_and_peer(d_s.reshape(dbs * dseq, POOL_WIDTH), attn_s, zs, xs, weights, 256)
    y_sample = y_sample.reshape(dbs, dseq, D_MODEL)
    pool_hist_sample = jnp.concatenate([state_pool, u_s], axis=1)[:, -POOL_HIST:]

    return (y_prompt, y_sample, k_win_prompt, v_win_prompt, pool_hist_prompt,
            k_win_sample, v_win_sample, pool_hist_sample)
```

```python
import functools
import math

import jax
import jax.numpy as jnp
from jax import lax
from jax.experimental import pallas as pl
from jax.experimental.pallas import tpu as pltpu

D_MODEL = 2048
N_HEADS = 16
N_KV_HEADS = 4
HEAD_DIM = 64
Q_GROUP = N_HEADS // N_KV_HEADS
ATTN_WIDTH = N_HEADS * HEAD_DIM
KV_WIDTH = N_KV_HEADS * HEAD_DIM
WINDOW = 128
ATTN_BLOCK = 128
ROPE_THETA = 10000.0
POOL_WINDOWS = (2, 4, 8, 16)
N_POOL_GROUPS = 4
POOL_WIDTH = D_MODEL // 2
POOL_GROUP = POOL_WIDTH // N_POOL_GROUPS
POOL_OUT_GROUP = D_MODEL // N_POOL_GROUPS
POOL_HIST = max(POOL_WINDOWS) - 1
HIST_ROWS = 16
IN_WIDTH = ATTN_WIDTH + 2 * KV_WIDTH + POOL_WIDTH + 2 * D_MODEL
COL = 512
PEER_HEADS = 8
N_KEYS = 128
N_EXPERTS = N_KEYS * N_KEYS
HALF_KEY = 128
PEER_TOPK = 16
PEER_PICKS = PEER_HEADS * PEER_TOPK
DEPTH = 1
ALPHA = (2.0 * DEPTH) ** 0.25
LN_EPS = 1e-5
NEG_INF = -1e30
PAST_LEN = 16384

VMEM_LIMIT = 56 * 1024 * 1024
BF16 = jnp.bfloat16
F32 = jnp.float32


def _cparams(sem):
    return pltpu.CompilerParams(dimension_semantics=sem, vmem_limit_bytes=VMEM_LIMIT)


def _resident(shape):
    zeros = (0,) * len(shape)
    return pl.BlockSpec(shape, lambda i: zeros, pipeline_mode=pl.Buffered(1))


def _in_proj_kernel(x_ref, w_ref, b_ref, o_ref):
    x = x_ref[...].astype(BF16)
    o_ref[...] = jnp.dot(x, w_ref[...], preferred_element_type=F32) + b_ref[...]


def _in_proj(x, w_bf, b, tm):
    m, k = x.shape
    n = w_bf.shape[1]
    return pl.pallas_call(
        _in_proj_kernel,
        out_shape=jax.ShapeDtypeStruct((m, n), F32),
        grid=(m // tm, n // COL),
        in_specs=[pl.BlockSpec((tm, k), lambda i, j: (i, 0)),
                  pl.BlockSpec((k, COL), lambda i, j: (0, j)),
                  pl.BlockSpec((1, COL), lambda i, j: (0, j))],
        out_specs=pl.BlockSpec((tm, COL), lambda i, j: (i, j)),
        compiler_params=_cparams(("arbitrary", "arbitrary")),
        name="in_proj",
    )(x, w_bf, b.reshape(1, n))


def _rope_tables(pos):
    half = HEAD_DIM // 2
    inv = ROPE_THETA ** (-jnp.arange(half, dtype=F32) / half)
    ang = pos.astype(F32)[:, None] * inv[None, :]
    cos = jnp.cos(ang)
    sin = jnp.sin(ang)
    cos_t = jnp.concatenate([cos, cos, cos, cos], axis=1)
    sin_t = jnp.concatenate([-sin, sin, -sin, sin], axis=1)
    return cos_t, sin_t


def _rope_apply(x, cos_t, sin_t):
    rows, w = x.shape
    reps = w // 128
    cos_f = jnp.concatenate([cos_t] * reps, axis=1) if reps > 1 else cos_t
    sin_f = jnp.concatenate([sin_t] * reps, axis=1) if reps > 1 else sin_t
    lane = lax.broadcasted_iota(jnp.int32, (rows, w), 1)
    first_half = (lane % HEAD_DIM) < (HEAD_DIM // 2)
    partner = jnp.where(first_half, pltpu.roll(x, w - HEAD_DIM // 2, axis=1),
                        pltpu.roll(x, HEAD_DIM // 2, axis=1))
    return x * cos_f + partner * sin_f


def _attn_kernel(sinks_ref, q_ref, kc_ref, kp_ref, vc_ref, vp_ref, cos_ref, sin_ref, cosp_ref, sinp_ref,
                 o_ref, krot_ref, *, blocks_per_seq, rotate_prev):
    j = pl.program_id(0) % blocks_per_seq
    has_prev = jnp.logical_or(j > 0, not rotate_prev)
    cos_t = cos_ref[...]
    sin_t = sin_ref[...]
    q = _rope_apply(q_ref[...], cos_t, sin_t)
    kc = _rope_apply(kc_ref[...], cos_t, sin_t)
    krot_ref[...] = kc
    if rotate_prev:
        kp = _rope_apply(kp_ref[...], cosp_ref[...], sinp_ref[...])
    else:
        kp = kp_ref[...]
    vc = vc_ref[...]
    vp = vp_ref[...]

    rows = Q_GROUP * ATTN_BLOCK
    qi = lax.broadcasted_iota(jnp.int32, (rows, 2 * ATTN_BLOCK), 0) % ATTN_BLOCK
    kj = lax.broadcasted_iota(jnp.int32, (rows, 2 * ATTN_BLOCK), 1)
    prev_lim = jnp.where(has_prev, ATTN_BLOCK, 0)
    valid = jnp.logical_or(jnp.logical_and(kj < prev_lim, kj > qi),
                           jnp.logical_and(kj >= ATTN_BLOCK, (kj - ATTN_BLOCK) <= qi))
    head_of_row = lax.broadcasted_iota(jnp.int32, (rows, 1), 0) // ATTN_BLOCK

    for g in range(N_KV_HEADS):
        lo = g * HEAD_DIM
        k_band = jnp.concatenate([kp[:, lo:lo + HEAD_DIM], kc[:, lo:lo + HEAD_DIM]], axis=0).astype(BF16)
        v_band = jnp.concatenate([vp[:, lo:lo + HEAD_DIM], vc[:, lo:lo + HEAD_DIM]], axis=0).astype(BF16)
        q_g = jnp.concatenate(
            [q[:, (g * Q_GROUP + h) * HEAD_DIM:(g * Q_GROUP + h + 1) * HEAD_DIM] for h in range(Q_GROUP)],
            axis=0).astype(BF16)
        s = lax.dot_general(q_g, k_band, (((1,), (1,)), ((), ())), preferred_element_type=F32)
        s = s * (HEAD_DIM ** -0.5)
        s = jnp.where(valid, s, NEG_INF)
        sink = jnp.zeros((rows, 1), F32)
        for h in range(Q_GROUP):
            sink = jnp.where(head_of_row == h, sinks_ref[g * Q_GROUP + h], sink)
        m = jnp.maximum(jnp.max(s, axis=-1, keepdims=True), sink)
        p = jnp.exp(s - m)
        denom = jnp.sum(p, axis=-1, keepdims=True) + jnp.exp(sink - m)
        p = (p / denom).astype(BF16)
        o = jnp.dot(p, v_band, preferred_element_type=F32)
        for h in range(Q_GROUP):
            c0 = (g * Q_GROUP + h) * HEAD_DIM
            o_ref[:, c0:c0 + HEAD_DIM] = o[h * ATTN_BLOCK:(h + 1) * ATTN_BLOCK, :]


def _attention(sinks, q_src, k_src, kp_src, v_src, vp_src, cos_t, sin_t, *, n_seq, blocks_per_seq,
               q_col, k_col, kp_col, v_col, vp_col, rotate_prev):
    nb = blocks_per_seq
    n_blocks = n_seq * nb
    blk = ATTN_BLOCK

    def prev_blk(i):
        if rotate_prev:
            return jnp.where(i % nb == 0, i, i - 1)
        return i

    def pos_blk(i):
        return i % nb

    def pos_prev_blk(i):
        return jnp.maximum(i % nb - 1, 0)

    kern = functools.partial(_attn_kernel, blocks_per_seq=nb, rotate_prev=rotate_prev)
    return pl.pallas_call(
        kern,
        out_shape=(jax.ShapeDtypeStruct((n_blocks * blk, ATTN_WIDTH), F32),
                   jax.ShapeDtypeStruct((n_seq * blk, KV_WIDTH), F32)),
        grid=(n_blocks,),
        in_specs=[pl.BlockSpec(memory_space=pltpu.SMEM),
                  pl.BlockSpec((blk, ATTN_WIDTH), lambda i: (i, q_col)),
                  pl.BlockSpec((blk, KV_WIDTH), lambda i: (i, k_col)),
                  pl.BlockSpec((blk, KV_WIDTH), lambda i: (prev_blk(i), kp_col)),
                  pl.BlockSpec((blk, KV_WIDTH), lambda i: (i, v_col)),
                  pl.BlockSpec((blk, KV_WIDTH), lambda i: (prev_blk(i), vp_col)),
                  pl.BlockSpec((blk, 128), lambda i: (pos_blk(i), 0)),
                  pl.BlockSpec((blk, 128), lambda i: (pos_blk(i), 0)),
                  pl.BlockSpec((blk, 128), lambda i: (pos_prev_blk(i), 0)),
                  pl.BlockSpec((blk, 128), lambda i: (pos_prev_blk(i), 0))],
        out_specs=(pl.BlockSpec((blk, ATTN_WIDTH), lambda i: (i, 0)),
                   pl.BlockSpec((blk, KV_WIDTH), lambda i: (i // nb, 0))),
        compiler_params=_cparams(("arbitrary",)),
        name="sink_attention",
    )(sinks, q_src, k_src, kp_src, v_src, vp_src, cos_t, sin_t, cos_t, sin_t)


def _pool_diff_kernel(hlo_ref, hhi_ref, clo_ref, chi_ref, o_ref, ext_ref, *, tiles_per_seq, first_pos, mask_first):
    j = pl.program_id(0) % tiles_per_seq
    bb, tm, _ = clo_ref.shape
    hist = jnp.concatenate([hlo_ref[...], hhi_ref[...]], axis=2)
    if mask_first:
        hist = jnp.where(j > 0, hist, 0.0)
    cur = jnp.concatenate([clo_ref[...], chi_ref[...]], axis=2)
    ext_ref[:, 0:HIST_ROWS, :] = hist
    ext_ref[:, HIST_ROWS:HIST_ROWS + tm, :] = cur
    pos = first_pos + j * tm + lax.broadcasted_iota(jnp.int32, (1, tm, 1), 1)
    for g, w in enumerate(POOL_WINDOWS):
        lo = g * POOL_GROUP
        wsum = ext_ref[:, HIST_ROWS:HIST_ROWS + tm, lo:lo + POOL_GROUP]
        for i in range(1, w):
            wsum = wsum + ext_ref[:, HIST_ROWS - i:HIST_ROWS - i + tm, lo:lo + POOL_GROUP]
        cnt = jnp.minimum(w, pos + 1).astype(F32)
        o_ref[:, :, lo:lo + POOL_GROUP] = wsum / cnt - ext_ref[:, HIST_ROWS:HIST_ROWS + tm, lo:lo + POOL_GROUP]


def _pool_diff(hist_src, cur_src, *, n_seq, seq_len, bb, tm, col0, first_pos, hist_from_cur):
    tiles = seq_len // tm
    hb = tm // HIST_ROWS if hist_from_cur else 1

    def hist_map(c):
        if hist_from_cur:
            return lambda i: (i // tiles, jnp.maximum((i % tiles) * hb - 1, 0), c)
        return lambda i: (i, 0, c)

    def cur_map(c):
        return lambda i: (i // tiles, i % tiles, c)

    hcol = col0 if hist_from_cur else 0
    kern = functools.partial(_pool_diff_kernel, tiles_per_seq=tiles, first_pos=first_pos,
                             mask_first=hist_from_cur)
    return pl.pallas_call(
        kern,
        out_shape=jax.ShapeDtypeStruct((n_seq, seq_len, POOL_WIDTH), F32),
        grid=((n_seq // bb) * tiles,),
        in_specs=[pl.BlockSpec((bb, HIST_ROWS, COL), hist_map(hcol)),
                  pl.BlockSpec((bb, HIST_ROWS, COL), hist_map(hcol + 1)),
                  pl.BlockSpec((bb, tm, COL), cur_map(col0)),
                  pl.BlockSpec((bb, tm, COL), cur_map(col0 + 1))],
        out_specs=pl.BlockSpec((bb, tm, POOL_WIDTH), lambda i: (i // tiles, i % tiles, 0)),
        scratch_shapes=[pltpu.VMEM((bb, HIST_ROWS + tm, POOL_WIDTH), F32)],
        compiler_params=_cparams(("arbitrary",)),
        name="pool_diff",
    )(hist_src, hist_src, cur_src, cur_src)


def _layer_norm_rows(x, g, b):
    mu = jnp.mean(x, axis=-1, keepdims=True)
    xc = x - mu
    var = jnp.mean(xc * xc, axis=-1, keepdims=True)
    return xc * lax.rsqrt(var + LN_EPS) * g + b


def _merge_kernel(*refs):
    (d_ref, a_ref, gp0, gp1, gp2, gp3, ga0, ga1, ga2, ga3, x_ref,
     wp_ref, ps_ref, wab_ref, wo_ref, g_ref, b_ref, h_ref) = refs
    gp = (gp0, gp1, gp2, gp3)
    ga = (ga0, ga1, ga2, ga3)
    attn = a_ref[...].astype(BF16)
    tok = None
    for c in range(N_POOL_GROUPS):
        d_c = d_ref[:, c * POOL_GROUP:(c + 1) * POOL_GROUP].astype(BF16)
        y_pool = jnp.dot(d_c, wp_ref[c], preferred_element_type=F32) * ps_ref[:, c * COL:(c + 1) * COL]
        attn_br = jnp.dot(attn, wab_ref[:, c * COL:(c + 1) * COL], preferred_element_type=F32)
        m_c = jax.nn.sigmoid(gp[c][...]) * y_pool + jax.nn.sigmoid(ga[c][...]) * attn_br
        part = jnp.dot(m_c.astype(BF16), wo_ref[c * COL:(c + 1) * COL, :], preferred_element_type=F32)
        tok = part if tok is None else tok + part
    h_ref[...] = _layer_norm_rows(ALPHA * x_ref[...] + tok, g_ref[...], b_ref[...])


def _merge(d, attn, z, x, wp_bf, pool_scale, wab_bf, wo_bf, ln_g, ln_b, tm):
    m = x.shape[0]
    gp_col = (ATTN_WIDTH + 2 * KV_WIDTH + POOL_WIDTH) // COL
    ga_col = gp_col + D_MODEL // COL
    z_specs = [pl.BlockSpec((tm, COL), (lambda i, c=c: (i, c))) for c in range(gp_col, gp_col + 4)]
    z_specs += [pl.BlockSpec((tm, COL), (lambda i, c=c: (i, c))) for c in range(ga_col, ga_col + 4)]
    return pl.pallas_call(
        _merge_kernel,
        out_shape=jax.ShapeDtypeStruct((m, D_MODEL), F32),
        grid=(m // tm,),
        in_specs=[pl.BlockSpec((tm, POOL_WIDTH), lambda i: (i, 0)),
                  pl.BlockSpec((tm, ATTN_WIDTH), lambda i: (i, 0))] + z_specs + [
                  pl.BlockSpec((tm, D_MODEL), lambda i: (i, 0)),
                  _resident((N_POOL_GROUPS, POOL_GROUP, POOL_OUT_GROUP)),
                  _resident((1, D_MODEL)),
                  _resident((ATTN_WIDTH, D_MODEL)),
                  _resident((D_MODEL, D_MODEL)),
                  _resident((1, D_MODEL)),
                  _resident((1, D_MODEL))],
        out_specs=pl.BlockSpec((tm, D_MODEL), lambda i: (i, 0)),
        compiler_params=_cparams(("arbitrary",)),
        name="merge_ln1",
    )(d, attn, *([z] * 8), x, wp_bf, pool_scale.reshape(1, D_MODEL), wab_bf, wo_bf,
      ln_g.reshape(1, D_MODEL), ln_b.reshape(1, D_MODEL))


def _topk_rows(s, order, payload, k):
    n, t = s.shape
    slot = lax.broadcasted_iota(jnp.int32, (k, t), 0)

    def body(j, carry):
        s, vals, picked = carry
        m = jnp.max(s, axis=0, keepdims=True)
        first = jnp.min(jnp.where(s == m, order, 1e9), axis=0, keepdims=True)
        hit = order == first
        if payload is None:
            pay = first
        else:
            pay = jnp.max(jnp.where(hit, payload, -1.0), axis=0, keepdims=True)
        vals = jnp.where(slot == j, m, vals)
        picked = jnp.where(slot == j, pay, picked)
        s = jnp.where(hit, -jnp.inf, s)
        return s, vals, picked

    _, vals, picked = lax.fori_loop(0, k, body, (s, jnp.zeros((k, t), F32), jnp.zeros((k, t), F32)))
    return vals, picked


CAND_HALF = PEER_TOPK // 2


def _candidates(first, second, scale):
    parts = [first[0:1, :] * scale + second]
    parts += [first[a:a + 1, :] * scale + second[0:CAND_HALF, :] for a in range(1, CAND_HALF)]
    parts.append(first[CAND_HALF:, :] * scale + second[0:1, :])
    return jnp.concatenate(parts, axis=0)


def _route_kernel(h_ref, wq_ref, keys_ref, idx_ref, gate_ref, q_scr):
    tm = h_ref.shape[0]
    q_scr[...] = jnp.dot(h_ref[...].astype(BF16), wq_ref[...], preferred_element_type=F32).astype(BF16)
    key_id = lax.broadcasted_iota(jnp.int32, (N_KEYS, tm), 0).astype(F32)
    r = lax.broadcasted_iota(jnp.int32, (PEER_TOPK + CAND_HALF * CAND_HALF, tm), 0)
    mid = r - PEER_TOPK
    pair_a = jnp.where(r < PEER_TOPK, 0, jnp.where(mid < (CAND_HALF - 1) * CAND_HALF, 1 + mid // CAND_HALF,
                                                   mid - (CAND_HALF - 1) * CAND_HALF + CAND_HALF))
    pair_b = jnp.where(r < PEER_TOPK, r, jnp.where(mid < (CAND_HALF - 1) * CAND_HALF, mid % CAND_HALF, 0))
    flat = (pair_a * PEER_TOPK + pair_b).astype(F32)
    for hd in range(PEER_HEADS):
        tops = []
        for p in range(2):
            c0 = (hd * 2 + p) * HALF_KEY
            s = lax.dot_general(keys_ref[hd * 2 + p], q_scr[:, c0:c0 + HALF_KEY],
                                (((1,), (1,)), ((), ())), preferred_element_type=F32)
            tops.append(_topk_rows(s, key_id, None, PEER_TOPK))
        (ts1, ti1), (ts2, ti2) = tops
        cand = _candidates(ts1, ts2, 1.0)
        eid = _candidates(ti1, ti2, float(N_KEYS))
        cs, ids = _topk_rows(cand, flat, eid, PEER_TOPK)
        e = jnp.exp(cs - jnp.max(cs, axis=0, keepdims=True))
        gate = e / jnp.sum(e, axis=0, keepdims=True)
        idx_ref[hd * PEER_TOPK:(hd + 1) * PEER_TOPK, :] = ids.astype(jnp.int32)
        gate_ref[hd * PEER_TOPK:(hd + 1) * PEER_TOPK, :] = gate


def _route(h, wq_bf, keys_bf, tm):
    m = h.shape[0]
    return pl.pallas_call(
        _route_kernel,
        out_shape=(jax.ShapeDtypeStruct((PEER_PICKS, m), jnp.int32),
                   jax.ShapeDtypeStruct((PEER_PICKS, m), F32)),
        grid=(m // tm,),
        in_specs=[pl.BlockSpec((tm, D_MODEL), lambda i: (i, 0)),
                  _resident((D_MODEL, PEER_HEADS * 2 * HALF_KEY)),
                  _resident((PEER_HEADS * 2, N_KEYS, HALF_KEY))],
        out_specs=(pl.BlockSpec((PEER_PICKS, tm), lambda i: (0, i)),
                   pl.BlockSpec((PEER_PICKS, tm), lambda i: (0, i))),
        scratch_shapes=[pltpu.VMEM((tm, PEER_HEADS * 2 * HALF_KEY), BF16)],
        compiler_params=_cparams(("arbitrary",)),
        name="peer_route",
    )(h, wq_bf, keys_bf)


PEER_TT = 8
SUBLANES = 8
PICK_TILES = PEER_PICKS // SUBLANES


def _gelu_exact(a):
    return 0.5 * a * (1.0 + lax.erf(a * (2.0 ** -0.5)))


def _peer_kernel(idx_ref, idxn_ref, h_ref, gate_ref, g_ref, b_ref, uv_hbm, y_ref, buf0, buf1, sem,
                 a_scr, o_scr, *, n_steps):
    i = pl.program_id(0)
    bufs = (buf0, buf1)
    tiles = PEER_TT * PICK_TILES

    def issue(ids_ref, t, dst, s, k_lo, k_hi):
        for k in range(k_lo, k_hi):
            e = ids_ref[t, k]
            pltpu.make_async_copy(uv_hbm.at[e >> 3, pl.ds(e & 7, 1)],
                                  dst.at[t * PICK_TILES + (k >> 3), pl.ds(k & 7, 1)],
                                  sem.at[s]).start()

    def wait_all(dst, s):
        pltpu.make_async_copy(uv_hbm.at[pl.ds(0, tiles)], dst, sem.at[s]).wait()

    @pl.when(i == 0)
    def _():
        a_scr[...] = jnp.zeros_like(a_scr)

        def prime(t, carry):
            issue(idx_ref, t, buf0, 0, 0, PEER_PICKS)
            return carry
        lax.fori_loop(0, PEER_TT, prime, 0)

    col0 = (i * PEER_TT) % 128
    lane = lax.broadcasted_iota(jnp.int32, (PEER_PICKS, 128), 1)

    def step(s):
        cur, nxt = bufs[s], bufs[1 - s]
        wait_all(cur, s)

        def dots(t, carry):
            issue(idxn_ref, t, nxt, 1 - s, 0, PEER_PICKS // 2)
            x = h_ref[pl.ds(t, 1), :]
            u = cur[pl.ds(t * PICK_TILES, PICK_TILES), :, 0:D_MODEL].reshape(PEER_PICKS, D_MODEL)
            a = jnp.sum(u * x, axis=1, keepdims=True)
            a_scr[...] = jnp.where(lane == col0 + t, a, a_scr[...])
            return carry
        lax.fori_loop(0, PEER_TT, dots, 0)

        a_scr[...] = _gelu_exact(a_scr[...]) * gate_ref[...]

        def mix(t, carry):
            issue(idxn_ref, t, nxt, 1 - s, PEER_PICKS // 2, PEER_PICKS)
            act = jnp.sum(jnp.where(lane == col0 + t, a_scr[...], 0.0), axis=1, keepdims=True)
            v = cur[pl.ds(t * PICK_TILES, PICK_TILES), :, D_MODEL:2 * D_MODEL].reshape(PEER_PICKS, D_MODEL)
            o_scr[pl.ds(t, 1), :] = jnp.sum(act * v, axis=0, keepdims=True)
            return carry
        lax.fori_loop(0, PEER_TT, mix, 0)

        y_ref[...] = _layer_norm_rows(ALPHA * h_ref[...] + o_scr[...], g_ref[...], b_ref[...])

        @pl.when(i == n_steps - 1)
        def _():
            wait_all(nxt, 1 - s)

    @pl.when(i % 2 == 0)
    def _():
        step(0)

    @pl.when(i % 2 == 1)
    def _():
        step(1)


def _peer_experts(idx_t, gate_t, h, uv, ln_g, ln_b):
    m = h.shape[0]
    n_steps = m // PEER_TT
    kern = functools.partial(_peer_kernel, n_steps=n_steps)
    uv3 = uv.reshape(N_EXPERTS // SUBLANES, SUBLANES, 2 * D_MODEL)
    buf = pltpu.VMEM((PEER_TT * PICK_TILES, SUBLANES, 2 * D_MODEL), F32)
    return pl.pallas_call(
        kern,
        out_shape=jax.ShapeDtypeStruct((m, D_MODEL), F32),
        grid=(n_steps,),
        in_specs=[pl.BlockSpec((PEER_TT, PEER_PICKS), lambda i: (i, 0), memory_space=pltpu.SMEM),
                  pl.BlockSpec((PEER_TT, PEER_PICKS), lambda i: (jnp.minimum(i + 1, n_steps - 1), 0),
                               memory_space=pltpu.SMEM),
                  pl.BlockSpec((PEER_TT, D_MODEL), lambda i: (i, 0)),
                  pl.BlockSpec((PEER_PICKS, 128), lambda i: (0, (i * PEER_TT) // 128)),
                  pl.BlockSpec((1, D_MODEL), lambda i: (0, 0)),
                  pl.BlockSpec((1, D_MODEL), lambda i: (0, 0)),
                  pl.BlockSpec(memory_space=pl.ANY)],
        out_specs=pl.BlockSpec((PEER_TT, D_MODEL), lambda i: (i, 0)),
        scratch_shapes=[buf, buf, pltpu.SemaphoreType.DMA((2,)),
                        pltpu.VMEM((PEER_PICKS, 128), F32), pltpu.VMEM((PEER_TT, D_MODEL), F32)],
        compiler_params=_cparams(("arbitrary",)),
        name="peer_experts",
    )(idx_t, idx_t, h, gate_t, ln_g.reshape(1, D_MODEL), ln_b.reshape(1, D_MODEL), uv3)


def _token_mixer_and_peer(d, attn, z, x, weights, tm):
    (wp_bf, pool_scale, wab_bf, wo_bf, ln1_g, ln1_b, wq_bf, keys_bf, uv, ln2_g, ln2_b) = weights
    tm = min(tm, x.shape[0])
    h = _merge(d, attn, z, x, wp_bf, pool_scale, wab_bf, wo_bf, ln1_g, ln1_b, tm)
    ids, gate_t = _route(h, wq_bf, keys_bf, tm)
    return _peer_experts(ids.T, gate_t, h, uv, ln2_g, ln2_b)


def kernel(x_prompt, x_sample, cache_k, cache_v, state_pool, w_in, b_in, attn_sinks, w_attn_br, w_pool, pool_scale, w_out, ln1_g, ln1_b, w_query, sub_keys, peer_u, peer_v, ln2_g, ln2_b):
    bsz, seq, _ = x_prompt.shape
    dbs, dseq, _ = x_sample.shape
    win = cache_k.shape[1]
    u_col = (ATTN_WIDTH + 2 * KV_WIDTH) // COL

    w_in_bf = w_in.astype(BF16)
    weights = (w_pool.astype(BF16), pool_scale, w_attn_br.astype(BF16), w_out.astype(BF16), ln1_g, ln1_b,
               w_query.astype(BF16), sub_keys.reshape(PEER_HEADS * 2, N_KEYS, HALF_KEY).astype(BF16),
               jnp.concatenate([peer_u, peer_v], axis=1), ln2_g, ln2_b)

    xp = x_prompt.reshape(bsz * seq, D_MODEL)
    z = _in_proj(xp, w_in_bf, b_in, min(512, bsz * seq))
    cos_t, sin_t = _rope_tables(jnp.arange(seq, dtype=jnp.int32))
    attn_p, k_last = _attention(attn_sinks, z, z, z, z, z, cos_t, sin_t, n_seq=bsz,
                                blocks_per_seq=seq // ATTN_BLOCK, q_col=0,
                                k_col=ATTN_WIDTH // KV_WIDTH, kp_col=ATTN_WIDTH // KV_WIDTH,
                                v_col=ATTN_WIDTH // KV_WIDTH + 1, vp_col=ATTN_WIDTH // KV_WIDTH + 1,
                                rotate_prev=True)
    z3 = z.reshape(bsz, seq, IN_WIDTH)
    d_p = _pool_diff(z3, z3, n_seq=bsz, seq_len=seq, bb=1, tm=256, col0=u_col, first_pos=0,
                     hist_from_cur=True)
    y_prompt = _token_mixer_and_peer(d_p.reshape(bsz * seq, POOL_WIDTH), attn_p, z, xp, weights, 256)
    y_prompt = y_prompt.reshape(bsz, seq, D_MODEL)
    k_win_prompt = k_last.reshape(bsz, ATTN_BLOCK, N_KV_HEADS, HEAD_DIM)
    v_win_prompt = z3[:, seq - WINDOW:, ATTN_WIDTH + KV_WIDTH:ATTN_WIDTH + 2 * KV_WIDTH].reshape(
        bsz, WINDOW, N_KV_HEADS, HEAD_DIM)
    pool_hist_prompt = z3[:, seq - POOL_HIST:, u_col * COL:u_col * COL + POOL_WIDTH]

    xs = x_sample.reshape(dbs * dseq, D_MODEL)
    zs = _in_proj(xs, w_in_bf, b_in, dbs * dseq)
    zs3 = zs.reshape(dbs, dseq, IN_WIDTH)
    qkv_pad = jnp.pad(zs3[:, :, :ATTN_WIDTH + 2 * KV_WIDTH], ((0, 0), (0, ATTN_BLOCK - dseq), (0, 0)))
    qkv_pad = qkv_pad.reshape(dbs * ATTN_BLOCK, ATTN_WIDTH + 2 * KV_WIDTH)
    q_pad = qkv_pad[:, :ATTN_WIDTH]
    kv_pad = qkv_pad[:, ATTN_WIDTH:]
    ck = cache_k.reshape(dbs * win, KV_WIDTH)
    cv = cache_v.reshape(dbs * win, KV_WIDTH)
    cos_s, sin_s = _rope_tables(PAST_LEN + jnp.arange(ATTN_BLOCK, dtype=jnp.int32))
    attn_s, k_new = _attention(attn_sinks, q_pad, kv_pad, ck, kv_pad, cv, cos_s, sin_s, n_seq=dbs,
                               blocks_per_seq=1, q_col=0, k_col=0, kp_col=0, v_col=1, vp_col=0,
                               rotate_prev=False)
    attn_s = attn_s.reshape(dbs, ATTN_BLOCK, ATTN_WIDTH)[:, :dseq].reshape(dbs * dseq, ATTN_WIDTH)
    k_new = k_new.reshape(dbs, ATTN_BLOCK, N_KV_HEADS, HEAD_DIM)[:, :dseq]
    v_new = zs3[:, :, ATTN_WIDTH + KV_WIDTH:ATTN_WIDTH + 2 * KV_WIDTH].reshape(dbs, dseq, N_KV_HEADS, HEAD_DIM)
    k_win_sample = jnp.concatenate([cache_k, k_new], axis=1)[:, -win:]
    v_win_sample = jnp.concatenate([cache_v, v_new], axis=1)[:, -win:]

    u_s = zs3[:, :, u_col * COL:u_col * COL + POOL_WIDTH]
    hist_s = jnp.pad(state_pool, ((0, 0), (HIST_ROWS - POOL_HIST, 0), (0, 0)))
    d_s = _pool_diff(hist_s, u_s, n_seq=dbs, seq_len=dseq, bb=8, tm=dseq, col0=0, first_pos=PAST_LEN,
                     hist_from_cur=False)
    y_sample = _token_mixer_and_peer(d_s.reshape(dbs * dseq, POOL_WIDTH), attn_s, zs, xs, weights, 256)
    y_sample = y_sample.reshape(dbs, dseq, D_MODEL)
    pool_hist_sample = jnp.concatenate([state_pool, u_s], axis=1)[:, -POOL_HIST:]

    return (y_prompt, y_sample, k_win_prompt, v_win_prompt, pool_hist_prompt,
            k_win_sample, v_win_sample, pool_hist_sample)
```

```python
import functools
import math

import jax
import jax.numpy as jnp
from jax import lax
from jax.experimental import pallas as pl
from jax.experimental.pallas import tpu as pltpu

D_MODEL = 2048
N_HEADS = 16
N_KV_HEADS = 4
HEAD_DIM = 64
Q_GROUP = N_HEADS // N_KV_HEADS
ATTN_WIDTH = N_HEADS * HEAD_DIM
KV_WIDTH = N_KV_HEADS * HEAD_DIM
WINDOW = 128
ATTN_BLOCK = 128
ROPE_THETA = 10000.0
POOL_WINDOWS = (2, 4, 8, 16)
N_POOL_GROUPS = 4
POOL_WIDTH = D_MODEL // 2
POOL_GROUP = POOL_WIDTH // N_POOL_GROUPS
POOL_OUT_GROUP = D_MODEL // N_POOL_GROUPS
POOL_HIST = max(POOL_WINDOWS) - 1
HIST_ROWS = 16
IN_WIDTH = ATTN_WIDTH + 2 * KV_WIDTH + POOL_WIDTH + 2 * D_MODEL
COL = 512
PEER_HEADS = 8
N_KEYS = 128
N_EXPERTS = N_KEYS * N_KEYS
HALF_KEY = 128
PEER_TOPK = 16
PEER_PICKS = PEER_HEADS * PEER_TOPK
DEPTH = 1
ALPHA = (2.0 * DEPTH) ** 0.25
LN_EPS = 1e-5
NEG_INF = -1e30
PAST_LEN = 16384

VMEM_LIMIT = 56 * 1024 * 1024
BF16 = jnp.bfloat16
F32 = jnp.float32


def _cparams(sem):
    return pltpu.CompilerParams(dimension_semantics=sem, vmem_limit_bytes=VMEM_LIMIT)


def _resident(shape):
    zeros = (0,) * len(shape)
    return pl.BlockSpec(shape, lambda i: zeros, pipeline_mode=pl.Buffered(1))


def _in_proj_kernel(x_ref, w_ref, b_ref, o_ref):
    x = x_ref[...].astype(BF16)
    o_ref[...] = jnp.dot(x, w_ref[...], preferred_element_type=F32) + b_ref[...]


def _in_proj(x, w_bf, b, tm):
    m, k = x.shape
    n = w_bf.shape[1]
    return pl.pallas_call(
        _in_proj_kernel,
        out_shape=jax.ShapeDtypeStruct((m, n), F32),
        grid=(m // tm, n // COL),
        in_specs=[pl.BlockSpec((tm, k), lambda i, j: (i, 0)),
                  pl.BlockSpec((k, COL), lambda i, j: (0, j)),
                  pl.BlockSpec((1, COL), lambda i, j: (0, j))],
        out_specs=pl.BlockSpec((tm, COL), lambda i, j: (i, j)),
        compiler_params=_cparams(("arbitrary", "arbitrary")),
        name="in_proj",
    )(x, w_bf, b.reshape(1, n))


def _rope_tables(pos):
    half = HEAD_DIM // 2
    inv = ROPE_THETA ** (-jnp.arange(half, dtype=F32) / half)
    ang = pos.astype(F32)[:, None] * inv[None, :]
    cos = jnp.cos(ang)
    sin = jnp.sin(ang)
    cos_t = jnp.concatenate([cos, cos, cos, cos], axis=1)
    sin_t = jnp.concatenate([-sin, sin, -sin, sin], axis=1)
    return cos_t, sin_t


def _rope_apply(x, cos_t, sin_t):
    rows, w = x.shape
    reps = w // 128
    cos_f = jnp.concatenate([cos_t] * reps, axis=1) if reps > 1 else cos_t
    sin_f = jnp.concatenate([sin_t] * reps, axis=1) if reps > 1 else sin_t
    lane = lax.broadcasted_iota(jnp.int32, (rows, w), 1)
    first_half = (lane % HEAD_DIM) < (HEAD_DIM // 2)
    partner = jnp.where(first_half, pltpu.roll(x, w - HEAD_DIM // 2, axis=1),
                        pltpu.roll(x, HEAD_DIM // 2, axis=1))
    return x * cos_f + partner * sin_f


def _attn_kernel(sinks_ref, q_ref, kc_ref, kp_ref, vc_ref, vp_ref, cos_ref, sin_ref, cosp_ref, sinp_ref,
                 o_ref, krot_ref, *, blocks_per_seq, rotate_prev):
    j = pl.program_id(0) % blocks_per_seq
    has_prev = jnp.logical_or(j > 0, not rotate_prev)
    cos_t = cos_ref[...]
    sin_t = sin_ref[...]
    q = _rope_apply(q_ref[...], cos_t, sin_t)
    kc = _rope_apply(kc_ref[...], cos_t, sin_t)
    krot_ref[...] = kc
    if rotate_prev:
        kp = _rope_apply(kp_ref[...], cosp_ref[...], sinp_ref[...])
    else:
        kp = kp_ref[...]
    vc = vc_ref[...]
    vp = vp_ref[...]

    rows = Q_GROUP * ATTN_BLOCK
    qi = lax.broadcasted_iota(jnp.int32, (rows, 2 * ATTN_BLOCK), 0) % ATTN_BLOCK
    kj = lax.broadcasted_iota(jnp.int32, (rows, 2 * ATTN_BLOCK), 1)
    prev_lim = jnp.where(has_prev, ATTN_BLOCK, 0)
    valid = jnp.logical_or(jnp.logical_and(kj < prev_lim, kj > qi),
                           jnp.logical_and(kj >= ATTN_BLOCK, (kj - ATTN_BLOCK) <= qi))
    head_of_row = lax.broadcasted_iota(jnp.int32, (rows, 1), 0) // ATTN_BLOCK

    for g in range(N_KV_HEADS):
        lo = g * HEAD_DIM
        k_band = jnp.concatenate([kp[:, lo:lo + HEAD_DIM], kc[:, lo:lo + HEAD_DIM]], axis=0).astype(BF16)
        v_band = jnp.concatenate([vp[:, lo:lo + HEAD_DIM], vc[:, lo:lo + HEAD_DIM]], axis=0).astype(BF16)
        q_g = jnp.concatenate(
            [q[:, (g * Q_GROUP + h) * HEAD_DIM:(g * Q_GROUP + h + 1) * HEAD_DIM] for h in range(Q_GROUP)],
            axis=0).astype(BF16)
        s = lax.dot_general(q_g, k_band, (((1,), (1,)), ((), ())), preferred_element_type=F32)
        s = s * (HEAD_DIM ** -0.5)
        s = jnp.where(valid, s, NEG_INF)
        sink = jnp.zeros((rows, 1), F32)
        for h in range(Q_GROUP):
            sink = jnp.where(head_of_row == h, sinks_ref[g * Q_GROUP + h], sink)
        m = jnp.maximum(jnp.max(s, axis=-1, keepdims=True), sink)
        p = jnp.exp(s - m)
        denom = jnp.sum(p, axis=-1, keepdims=True) + jnp.exp(sink - m)
        p = (p / denom).astype(BF16)
        o = jnp.dot(p, v_band, preferred_element_type=F32)
        for h in range(Q_GROUP):
            c0 = (g * Q_GROUP + h) * HEAD_DIM
            o_ref[:, c0:c0 + HEAD_DIM] = o[h * ATTN_BLOCK:(h + 1) * ATTN_BLOCK, :]


def _attention(sinks, q_src, k_src, kp_src, v_src, vp_src, cos_t, sin_t, *, n_seq, blocks_per_seq,
               q_col, k_col, kp_col, v_col, vp_col, rotate_prev):
    nb = blocks_per_seq
    n_blocks = n_seq * nb
    blk = ATTN_BLOCK

    def prev_blk(i):
        if rotate_prev:
            return jnp.where(i % nb == 0, i, i - 1)
        return i

    def pos_blk(i):
        return i % nb

    def pos_prev_blk(i):
        return jnp.maximum(i % nb - 1, 0)

    kern = functools.partial(_attn_kernel, blocks_per_seq=nb, rotate_prev=rotate_prev)
    return pl.pallas_call(
        kern,
        out_shape=(jax.ShapeDtypeStruct((n_blocks * blk, ATTN_WIDTH), F32),
                   jax.ShapeDtypeStruct((n_seq * blk, KV_WIDTH), F32)),
        grid=(n_blocks,),
        in_specs=[pl.BlockSpec(memory_space=pltpu.SMEM),
                  pl.BlockSpec((blk, ATTN_WIDTH), lambda i: (i, q_col)),
                  pl.BlockSpec((blk, KV_WIDTH), lambda i: (i, k_col)),
                  pl.BlockSpec((blk, KV_WIDTH), lambda i: (prev_blk(i), kp_col)),
                  pl.BlockSpec((blk, KV_WIDTH), lambda i: (i, v_col)),
                  pl.BlockSpec((blk, KV_WIDTH), lambda i: (prev_blk(i), vp_col)),
                  pl.BlockSpec((blk, 128), lambda i: (pos_blk(i), 0)),
                  pl.BlockSpec((blk, 128), lambda i: (pos_blk(i), 0)),
                  pl.BlockSpec((blk, 128), lambda i: (pos_prev_blk(i), 0)),
                  pl.BlockSpec((blk, 128), lambda i: (pos_prev_blk(i), 0))],
        out_specs=(pl.BlockSpec((blk, ATTN_WIDTH), lambda i: (i, 0)),
                   pl.BlockSpec((blk, KV_WIDTH), lambda i: (i // nb, 0))),
        compiler_params=_cparams(("arbitrary",)),
        name="sink_attention",
    )(sinks, q_src, k_src, kp_src, v_src, vp_src, cos_t, sin_t, cos_t, sin_t)


def _pool_diff_kernel(hlo_ref, hhi_ref, clo_ref, chi_ref, o_ref, ext_ref, *, tiles_per_seq, first_pos, mask_first):
    j = pl.program_id(0) % tiles_per_seq
    bb, tm, _ = clo_ref.shape
    hist = jnp.concatenate([hlo_ref[...], hhi_ref[...]], axis=2)
    if mask_first:
        hist = jnp.where(j > 0, hist, 0.0)
    cur = jnp.concatenate([clo_ref[...], chi_ref[...]], axis=2)
    ext_ref[:, 0:HIST_ROWS, :] = hist
    ext_ref[:, HIST_ROWS:HIST_ROWS + tm, :] = cur
    pos = first_pos + j * tm + lax.broadcasted_iota(jnp.int32, (1, tm, 1), 1)
    for g, w in enumerate(POOL_WINDOWS):
        lo = g * POOL_GROUP
        wsum = ext_ref[:, HIST_ROWS:HIST_ROWS + tm, lo:lo + POOL_GROUP]
        for i in range(1, w):
            wsum = wsum + ext_ref[:, HIST_ROWS - i:HIST_ROWS - i + tm, lo:lo + POOL_GROUP]
        cnt = jnp.minimum(w, pos + 1).astype(F32)
        o_ref[:, :, lo:lo + POOL_GROUP] = wsum / cnt - ext_ref[:, HIST_ROWS:HIST_ROWS + tm, lo:lo + POOL_GROUP]


def _pool_diff(hist_src, cur_src, *, n_seq, seq_len, bb, tm, col0, first_pos, hist_from_cur):
    tiles = seq_len // tm
    hb = tm // HIST_ROWS if hist_from_cur else 1

    def hist_map(c):
        if hist_from_cur:
            return lambda i: (i // tiles, jnp.maximum((i % tiles) * hb - 1, 0), c)
        return lambda i: (i, 0, c)

    def cur_map(c):
        return lambda i: (i // tiles, i % tiles, c)

    hcol = col0 if hist_from_cur else 0
    kern = functools.partial(_pool_diff_kernel, tiles_per_seq=tiles, first_pos=first_pos,
                             mask_first=hist_from_cur)
    return pl.pallas_call(
        kern,
        out_shape=jax.ShapeDtypeStruct((n_seq, seq_len, POOL_WIDTH), F32),
        grid=((n_seq // bb) * tiles,),
        in_specs=[pl.BlockSpec((bb, HIST_ROWS, COL), hist_map(hcol)),
                  pl.BlockSpec((bb, HIST_ROWS, COL), hist_map(hcol + 1)),
                  pl.BlockSpec((bb, tm, COL), cur_map(col0)),
                  pl.BlockSpec((bb, tm, COL), cur_map(col0 + 1))],
        out_specs=pl.BlockSpec((bb, tm, POOL_WIDTH), lambda i: (i // tiles, i % tiles, 0)),
        scratch_shapes=[pltpu.VMEM((bb, HIST_ROWS + tm, POOL_WIDTH), F32)],
        compiler_params=_cparams(("arbitrary",)),
        name="pool_diff",
    )(hist_src, hist_src, cur_src, cur_src)


def _layer_norm_rows(x, g, b):
    mu = jnp.mean(x, axis=-1, keepdims=True)
    xc = x - mu
    var = jnp.mean(xc * xc, axis=-1, keepdims=True)
    return xc * lax.rsqrt(var + LN_EPS) * g + b


def _merge_kernel(*refs):
    (d_ref, a_ref, gp0, gp1, gp2, gp3, ga0, ga1, ga2, ga3, x_ref,
     wp_ref, ps_ref, wab_ref, wo_ref, g_ref, b_ref, h_ref) = refs
    gp = (gp0, gp1, gp2, gp3)
    ga = (ga0, ga1, ga2, ga3)
    attn = a_ref[...].astype(BF16)
    tok = None
    for c in range(N_POOL_GROUPS):
        d_c = d_ref[:, c * POOL_GROUP:(c + 1) * POOL_GROUP].astype(BF16)
        y_pool = jnp.dot(d_c, wp_ref[c], preferred_element_type=F32) * ps_ref[:, c * COL:(c + 1) * COL]
        attn_br = jnp.dot(attn, wab_ref[:, c * COL:(c + 1) * COL], preferred_element_type=F32)
        m_c = jax.nn.sigmoid(gp[c][...]) * y_pool + jax.nn.sigmoid(ga[c][...]) * attn_br
        part = jnp.dot(m_c.astype(BF16), wo_ref[c * COL:(c + 1) * COL, :], preferred_element_type=F32)
        tok = part if tok is None else tok + part
    h_ref[...] = _layer_norm_rows(ALPHA * x_ref[...] + tok, g_ref[...], b_ref[...])


def _merge(d, attn, z, x, wp_bf, pool_scale, wab_bf, wo_bf, ln_g, ln_b, tm):
    m = x.shape[0]
    gp_col = (ATTN_WIDTH + 2 * KV_WIDTH + POOL_WIDTH) // COL
    ga_col = gp_col + D_MODEL // COL
    z_specs = [pl.BlockSpec((tm, COL), (lambda i, c=c: (i, c))) for c in range(gp_col, gp_col + 4)]
    z_specs += [pl.BlockSpec((tm, COL), (lambda i, c=c: (i, c))) for c in range(ga_col, ga_col + 4)]
    return pl.pallas_call(
        _merge_kernel,
        out_shape=jax.ShapeDtypeStruct((m, D_MODEL), F32),
        grid=(m // tm,),
        in_specs=[pl.BlockSpec((tm, POOL_WIDTH), lambda i: (i, 0)),
                  pl.BlockSpec((tm, ATTN_WIDTH), lambda i: (i, 0))] + z_specs + [
                  pl.BlockSpec((tm, D_MODEL), lambda i: (i, 0)),
                  _resident((N_POOL_GROUPS, POOL_GROUP, POOL_OUT_GROUP)),
                  _resident((1, D_MODEL)),
                  _resident((ATTN_WIDTH, D_MODEL)),
                  _resident((D_MODEL, D_MODEL)),
                  _resident((1, D_MODEL)),
                  _resident((1, D_MODEL))],
        out_specs=pl.BlockSpec((tm, D_MODEL), lambda i: (i, 0)),
        compiler_params=_cparams(("arbitrary",)),
        name="merge_ln1",
    )(d, attn, *([z] * 8), x, wp_bf, pool_scale.reshape(1, D_MODEL), wab_bf, wo_bf,
      ln_g.reshape(1, D_MODEL), ln_b.reshape(1, D_MODEL))


def _topk_rows(s, order, payload, k):
    n, t = s.shape
    slot = lax.broadcasted_iota(jnp.int32, (k, t), 0)

    def body(j, carry):
        s, vals, picked = carry
        m = jnp.max(s, axis=0, keepdims=True)
        first = jnp.min(jnp.where(s == m, order, 1e9), axis=0, keepdims=True)
        hit = order == first
        if payload is None:
            pay = first
        else:
            pay = jnp.max(jnp.where(hit, payload, -1.0), axis=0, keepdims=True)
        vals = jnp.where(slot == j, m, vals)
        picked = jnp.where(slot == j, pay, picked)
        s = jnp.where(hit, -jnp.inf, s)
        return s, vals, picked

    _, vals, picked = lax.fori_loop(0, k, body, (s, jnp.zeros((k, t), F32), jnp.zeros((k, t), F32)))
    return vals, picked


CAND_HALF = PEER_TOPK // 2


def _candidates(first, second, scale):
    parts = [first[0:1, :] * scale + second]
    parts += [first[a:a + 1, :] * scale + second[0:CAND_HALF, :] for a in range(1, CAND_HALF)]
    parts.append(first[CAND_HALF:, :] * scale + second[0:1, :])
    return jnp.concatenate(parts, axis=0)


def _route_kernel(h_ref, wq_ref, keys_ref, idx_ref, gate_ref, q_scr):
    tm = h_ref.shape[0]
    q_scr[...] = jnp.dot(h_ref[...].astype(BF16), wq_ref[...], preferred_element_type=F32).astype(BF16)
    key_id = lax.broadcasted_iota(jnp.int32, (N_KEYS, tm), 0).astype(F32)
    r = lax.broadcasted_iota(jnp.int32, (PEER_TOPK + CAND_HALF * CAND_HALF, tm), 0)
    mid = r - PEER_TOPK
    pair_a = jnp.where(r < PEER_TOPK, 0, jnp.where(mid < (CAND_HALF - 1) * CAND_HALF, 1 + mid // CAND_HALF,
                                                   mid - (CAND_HALF - 1) * CAND_HALF + CAND_HALF))
    pair_b = jnp.where(r < PEER_TOPK, r, jnp.where(mid < (CAND_HALF - 1) * CAND_HALF, mid % CAND_HALF, 0))
    flat = (pair_a * PEER_TOPK + pair_b).astype(F32)
    for hd in range(PEER_HEADS):
        tops = []
        for p in range(2):
            c0 = (hd * 2 + p) * HALF_KEY
            s = lax.dot_general(keys_ref[hd * 2 + p], q_scr[:, c0:c0 + HALF_KEY],
                                (((1,), (1,)), ((), ())), preferred_element_type=F32)
            tops.append(_topk_rows(s, key_id, None, PEER_TOPK))
        (ts1, ti1), (ts2, ti2) = tops
        cand = _candidates(ts1, ts2, 1.0)
        eid = _candidates(ti1, ti2, float(N_KEYS))
        cs, ids = _topk_rows(cand, flat, eid, PEER_TOPK)
        e = jnp.exp(cs - jnp.max(cs, axis=0, keepdims=True))
        gate = e / jnp.sum(e, axis=0, keepdims=True)
        idx_ref[hd * PEER_TOPK:(hd + 1) * PEER_TOPK, :] = ids.astype(jnp.int32)
        gate_ref[hd * PEER_TOPK:(hd + 1) * PEER_TOPK, :] = gate


def _route(h, wq_bf, keys_bf, tm):
    m = h.shape[0]
    return pl.pallas_call(
        _route_kernel,
        out_shape=(jax.ShapeDtypeStruct((PEER_PICKS, m), jnp.int32),
                   jax.ShapeDtypeStruct((PEER_PICKS, m), F32)),
        grid=(m // tm,),
        in_specs=[pl.BlockSpec((tm, D_MODEL), lambda i: (i, 0)),
                  _resident((D_MODEL, PEER_HEADS * 2 * HALF_KEY)),
                  _resident((PEER_HEADS * 2, N_KEYS, HALF_KEY))],
        out_specs=(pl.BlockSpec((PEER_PICKS, tm), lambda i: (0, i)),
                   pl.BlockSpec((PEER_PICKS, tm), lambda i: (0, i))),
        scratch_shapes=[pltpu.VMEM((tm, PEER_HEADS * 2 * HALF_KEY), BF16)],
        compiler_params=_cparams(("arbitrary",)),
        name="peer_route",
    )(h, wq_bf, keys_bf)


PEER_TT = 8
SUBLANES = 8
PICK_TILES = PEER_PICKS // SUBLANES


def _gelu_exact(a):
    return 0.5 * a * (1.0 + lax.erf(a * (2.0 ** -0.5)))


def _peer_kernel(idx_ref, idxn_ref, h_ref, gate_ref, g_ref, b_ref, uv_hbm, y_ref, buf0, buf1, sem,
                 a_scr, o_scr, *, n_steps):
    i = pl.program_id(0)
    bufs = (buf0, buf1)
    tiles = PEER_TT * PICK_TILES

    def issue(ids_ref, t, dst, s, k_lo, k_hi):
        for k in range(k_lo, k_hi):
            e = ids_ref[t, k]
            pltpu.make_async_copy(uv_hbm.at[e >> 3, pl.ds(e & 7, 1)],
                                  dst.at[t * PICK_TILES + (k >> 3), pl.ds(k & 7, 1)],
                                  sem.at[s]).start(priority=k & 1)

    def wait_all(dst, s):
        pltpu.make_async_copy(uv_hbm.at[pl.ds(0, tiles)], dst, sem.at[s]).wait()

    @pl.when(i == 0)
    def _():
        a_scr[...] = jnp.zeros_like(a_scr)

        def prime(t, carry):
            issue(idx_ref, t, buf0, 0, 0, PEER_PICKS)
            return carry
        lax.fori_loop(0, PEER_TT, prime, 0)

    col0 = (i * PEER_TT) % 128
    lane = lax.broadcasted_iota(jnp.int32, (PEER_PICKS, 128), 1)

    def step(s):
        cur, nxt = bufs[s], bufs[1 - s]
        wait_all(cur, s)

        def dots(t, carry):
            issue(idxn_ref, t, nxt, 1 - s, 0, PEER_PICKS // 2)
            x = h_ref[pl.ds(t, 1), :]
            u = cur[pl.ds(t * PICK_TILES, PICK_TILES), :, 0:D_MODEL].reshape(PEER_PICKS, D_MODEL)
            a = jnp.sum(u * x, axis=1, keepdims=True)
            a_scr[...] = jnp.where(lane == col0 + t, a, a_scr[...])
            return carry
        lax.fori_loop(0, PEER_TT, dots, 0)

        a_scr[...] = _gelu_exact(a_scr[...]) * gate_ref[...]

        def mix(t, carry):
            issue(idxn_ref, t, nxt, 1 - s, PEER_PICKS // 2, PEER_PICKS)
            act = jnp.sum(jnp.where(lane == col0 + t, a_scr[...], 0.0), axis=1, keepdims=True)
            v = cur[pl.ds(t * PICK_TILES, PICK_TILES), :, D_MODEL:2 * D_MODEL].reshape(PEER_PICKS, D_MODEL)
            o_scr[pl.ds(t, 1), :] = jnp.sum(act * v, axis=0, keepdims=True)
            return carry
        lax.fori_loop(0, PEER_TT, mix, 0)

        y_ref[...] = _layer_norm_rows(ALPHA * h_ref[...] + o_scr[...], g_ref[...], b_ref[...])

        @pl.when(i == n_steps - 1)
        def _():
            wait_all(nxt, 1 - s)

    @pl.when(i % 2 == 0)
    def _():
        step(0)

    @pl.when(i % 2 == 1)
    def _():
        step(1)


def _peer_experts(idx_t, gate_t, h, uv, ln_g, ln_b):
    m = h.shape[0]
    n_steps = m // PEER_TT
    kern = functools.partial(_peer_kernel, n_steps=n_steps)
    uv3 = uv.reshape(N_EXPERTS // SUBLANES, SUBLANES, 2 * D_MODEL)
    buf = pltpu.VMEM((PEER_TT * PICK_TILES, SUBLANES, 2 * D_MODEL), F32)
    return pl.pallas_call(
        kern,
        out_shape=jax.ShapeDtypeStruct((m, D_MODEL), F32),
        grid=(n_steps,),
        in_specs=[pl.BlockSpec((PEER_TT, PEER_PICKS), lambda i: (i, 0), memory_space=pltpu.SMEM),
                  pl.BlockSpec((PEER_TT, PEER_PICKS), lambda i: (jnp.minimum(i + 1, n_steps - 1), 0),
                               memory_space=pltpu.SMEM),
                  pl.BlockSpec((PEER_TT, D_MODEL), lambda i: (i, 0)),
                  pl.BlockSpec((PEER_PICKS, 128), lambda i: (0, (i * PEER_TT) // 128)),
                  pl.BlockSpec((1, D_MODEL), lambda i: (0, 0)),
                  pl.BlockSpec((1, D_MODEL), lambda i: (0, 0)),
                  pl.BlockSpec(memory_space=pl.ANY)],
        out_specs=pl.BlockSpec((PEER_TT, D_MODEL), lambda i: (i, 0)),
        scratch_shapes=[buf, buf, pltpu.SemaphoreType.DMA((2,)),
                        pltpu.VMEM((PEER_PICKS, 128), F32), pltpu.VMEM((PEER_TT, D_MODEL), F32)],
        compiler_params=_cparams(("arbitrary",)),
        name="peer_experts",
    )(idx_t, idx_t, h, gate_t, ln_g.reshape(1, D_MODEL), ln_b.reshape(1, D_MODEL), uv3)


def _token_mixer_and_peer(d, attn, z, x, weights, tm):
    (wp_bf, pool_scale, wab_bf, wo_bf, ln1_g, ln1_b, wq_bf, keys_bf, uv, ln2_g, ln2_b) = weights
    tm = min(tm, x.shape[0])
    h = _merge(d, attn, z, x, wp_bf, pool_scale, wab_bf, wo_bf, ln1_g, ln1_b, tm)
    ids, gate_t = _route(h, wq_bf, keys_bf, tm)
    return _peer_experts(ids.T, gate_t, h, uv, ln2_g, ln2_b)


def kernel(x_prompt, x_sample, cache_k, cache_v, state_pool, w_in, b_in, attn_sinks, w_attn_br, w_pool, pool_scale, w_out, ln1_g, ln1_b, w_query, sub_keys, peer_u, peer_v, ln2_g, ln2_b):
    bsz, seq, _ = x_prompt.shape
    dbs, dseq, _ = x_sample.shape
    win = cache_k.shape[1]
    u_col = (ATTN_WIDTH + 2 * KV_WIDTH) // COL

    w_in_bf = w_in.astype(BF16)
    weights = (w_pool.astype(BF16), pool_scale, w_attn_br.astype(BF16), w_out.astype(BF16), ln1_g, ln1_b,
               w_query.astype(BF16), sub_keys.reshape(PEER_HEADS * 2, N_KEYS, HALF_KEY).astype(BF16),
               jnp.concatenate([peer_u, peer_v], axis=1), ln2_g, ln2_b)

    xp = x_prompt.reshape(bsz * seq, D_MODEL)
    z = _in_proj(xp, w_in_bf, b_in, min(512, bsz * seq))
    cos_t, sin_t = _rope_tables(jnp.arange(seq, dtype=jnp.int32))
    attn_p, k_last = _attention(attn_sinks, z, z, z, z, z, cos_t, sin_t, n_seq=bsz,
                                blocks_per_seq=seq // ATTN_BLOCK, q_col=0,
                                k_col=ATTN_WIDTH // KV_WIDTH, kp_col=ATTN_WIDTH // KV_WIDTH,
                                v_col=ATTN_WIDTH // KV_WIDTH + 1, vp_col=ATTN_WIDTH // KV_WIDTH + 1,
                                rotate_prev=True)
    z3 = z.reshape(bsz, seq, IN_WIDTH)
    d_p = _pool_diff(z3, z3, n_seq=bsz, seq_len=seq, bb=1, tm=256, col0=u_col, first_pos=0,
                     hist_from_cur=True)
    y_prompt = _token_mixer_and_peer(d_p.reshape(bsz * seq, POOL_WIDTH), attn_p, z, xp, weights, 256)
    y_prompt = y_prompt.reshape(bsz, seq, D_MODEL)
    k_win_prompt = k_last.reshape(bsz, ATTN_BLOCK, N_KV_HEADS, HEAD_DIM)
    v_win_prompt = z3[:, seq - WINDOW:, ATTN_WIDTH + KV_WIDTH:ATTN_WIDTH + 2 * KV_WIDTH].reshape(
        bsz, WINDOW, N_KV_HEADS, HEAD_DIM)
    pool_hist_prompt = z3[:, seq - POOL_HIST:, u_col * COL:u_col * COL + POOL_WIDTH]

    xs = x_sample.reshape(dbs * dseq, D_MODEL)
    zs = _in_proj(xs, w_in_bf, b_in, dbs * dseq)
    zs3 = zs.reshape(dbs, dseq, IN_WIDTH)
    qkv_pad = jnp.pad(zs3[:, :, :ATTN_WIDTH + 2 * KV_WIDTH], ((0, 0), (0, ATTN_BLOCK - dseq), (0, 0)))
    qkv_pad = qkv_pad.reshape(dbs * ATTN_BLOCK, ATTN_WIDTH + 2 * KV_WIDTH)
    q_pad = qkv_pad[:, :ATTN_WIDTH]
    kv_pad = qkv_pad[:, ATTN_WIDTH:]
    ck = cache_k.reshape(dbs * win, KV_WIDTH)
    cv = cache_v.reshape(dbs * win, KV_WIDTH)
    cos_s, sin_s = _rope_tables(PAST_LEN + jnp.arange(ATTN_BLOCK, dtype=jnp.int32))
    attn_s, k_new = _attention(attn_sinks, q_pad, kv_pad, ck, kv_pad, cv, cos_s, sin_s, n_seq=dbs,
                               blocks_per_seq=1, q_col=0, k_col=0, kp_col=0, v_col=1, vp_col=0,
                               rotate_prev=False)
    attn_s = attn_s.reshape(dbs, ATTN_BLOCK, ATTN_WIDTH)[:, :dseq].reshape(dbs * dseq, ATTN_WIDTH)
    k_new = k_new.reshape(dbs, ATTN_BLOCK, N_KV_HEADS, HEAD_DIM)[:, :dseq]
    v_new = zs3[:, :, ATTN_WIDTH + KV_WIDTH:ATTN_WIDTH + 2 * KV_WIDTH].reshape(dbs, dseq, N_KV_HEADS, HEAD_DIM)
    k_win_sample = jnp.concatenate([cache_k, k_new], axis=1)[:, -win:]
    v_win_sample = jnp.concatenate([cache_v, v_new], axis=1)[:, -win:]

    u_s = zs3[:, :, u_col * COL:u_col * COL + POOL_WIDTH]
    hist_s = jnp.pad(state_pool, ((0, 0), (HIST_ROWS - POOL_HIST, 0), (0, 0)))
    d_s = _pool_diff(hist_s, u_s, n_seq=dbs, seq_len=dseq, bb=8, tm=dseq, col0=0, first_pos=PAST_LEN,
                     hist_from_cur=False)
    y_sample = _token_mixer_and_peer(d_s.reshape(dbs * dseq, POOL_WIDTH), attn_s, zs, xs, weights, 256)
    y_sample = y_sample.reshape(dbs, dseq, D_MODEL)
    pool_hist_sample = jnp.concatenate([state_pool, u_s], axis=1)[:, -POOL_HIST:]

    return (y_prompt, y_sample, k_win_prompt, v_win_prompt, pool_hist_prompt,
            k_win_sample, v_win_sample, pool_hist_sample)
```
